```python
import jax, jax.numpy as jnp
from jax import lax
import numpy as np

D_MODEL = 4096
BATCH = 4
SEQ = 2048
DEPTH = 2
DEC_BATCH = 128
DEC_SEQ = 4
PAST_LEN = 16384
PAGE_SIZE = 128

M_HEADS = 8
M_QK = 256
M_V = D_MODEL // M_HEADS
M_QK_W = M_HEADS * M_QK
M_V_W = M_HEADS * M_V
CONV_W = 4
M_CHUNK = 128
G_GROUPS = 8
G_CHUNK = 128
G_W = D_MODEL
G_GD = G_W // G_GROUPS
P_HEADS = 8
P_NKEYS = 128
P_EXPERTS = P_NKEYS * P_NKEYS
P_TOPK = 16
P_QDIM = 256
P_HALF = P_QDIM // 2
P_BLOCK = 64
IN_SPLITS = (M_QK_W, M_V_W, M_V_W, M_HEADS, M_HEADS, G_W, G_W, D_MODEL, D_MODEL)
IN_W = M_QK_W + 2 * M_V_W + 2 * M_HEADS + 2 * G_W + 2 * D_MODEL
N_MOD = 6
EPS = 1e-6
NEG_BIG = -1e30

kernel_name = "hybrid_mlstm_chunkmlp_peer_step"


def rms_norm(x, g):
    xf = x.astype(jnp.float32)
    y = xf * lax.rsqrt(jnp.mean(xf * xf, -1, keepdims=True) + EPS) * g.astype(jnp.float32)
    return y.astype(x.dtype)


def layer_norm(x, g, b):
    xf = x.astype(jnp.float32)
    mu = jnp.mean(xf, -1, keepdims=True)
    xc = xf - mu
    var = jnp.mean(xc * xc, -1, keepdims=True)
    return xc * lax.rsqrt(var + EPS) * g.astype(jnp.float32) + b.astype(jnp.float32)


def split_in(z):
    parts, o = [], 0
    for w in IN_SPLITS:
        parts.append(z[..., o:o + w])
        o += w
    return parts


def causal_conv(x_pad, w, b, L):
    out = b
    for j in range(CONV_W):
        out = out + x_pad[:, j:j + L] * w[j]
    return out


def mlstm_chunk(carry, inp):
    C0, n0, m0 = carry
    q, k, v, ig, lf = inp
    c = q.shape[1]
    F = jnp.cumsum(lf, axis=1)
    m = F + jnp.maximum(m0[:, None], lax.cummax(ig - F, axis=1))
    mask = jnp.tril(jnp.ones((c, c), bool))[None, :, :, None]
    logD = F[:, :, None] - F[:, None, :] + ig[:, None, :] - m[:, :, None]
    D = jnp.exp(jnp.where(mask, logD, -jnp.inf))
    a = jnp.exp(F + m0[:, None] - m)
    S = jnp.einsum('bthd,bshd->btsh', q, k) * D
    num = a[..., None] * jnp.einsum('bthk,bhkv->bthv', q, C0) + jnp.einsum('btsh,bshv->bthv', S, v)
    dot = a * jnp.einsum('bthk,bhk->bth', q, n0) + S.sum(2)
    den = jnp.maximum(jnp.abs(dot), jnp.exp(-m)) + EPS
    h = num / den[..., None]
    mL = m[:, -1]
    FL = F[:, -1]
    ws = jnp.exp(FL[:, None] - F + ig - mL[:, None])
    aL = jnp.exp(FL + m0 - mL)
    C1 = aL[..., None, None] * C0 + jnp.einsum('bsh,bshk,bshv->bhkv', ws, k, v)
    n1 = aL[..., None] * n0 + jnp.einsum('bsh,bshk->bhk', ws, k)
    return (C1, n1, mL), h


def mlstm_scan(q, k, v, ig, lf, C0, n0, m0):
    B, L = q.shape[:2]
    c = min(M_CHUNK, L)
    nC = -(-L // c)
    pad = nC * c - L

    def padt(a, val):
        return jnp.pad(a, [(0, 0), (0, pad)] + [(0, 0)] * (a.ndim - 2), constant_values=val)

    def to_chunks(a):
        return jnp.moveaxis(a.reshape((B, nC, c) + a.shape[2:]), 1, 0)

    xs = (to_chunks(padt(q, 0.0)), to_chunks(padt(k, 0.0)), to_chunks(padt(v, 0.0)),
          to_chunks(padt(ig, NEG_BIG)), to_chunks(padt(lf, 0.0)))
    (C1, n1, m1), h = lax.scan(mlstm_chunk, (C0, n0, m0), xs)
    h = jnp.moveaxis(h, 0, 1).reshape(B, nC * c, M_HEADS, M_V)[:, :L]
    return h, C1, n1, m1


def spatial_mix(vn, w_s, b_s):
    B, L, _ = vn.shape
    nC = -(-L // G_CHUNK)
    vp = jnp.pad(vn, ((0, 0), (0, nC * G_CHUNK - L), (0, 0))).reshape(B, nC, G_CHUNK, G_GROUPS, G_GD)
    w = jnp.where(jnp.tril(jnp.ones((G_CHUNK, G_CHUNK), bool)), w_s.astype(jnp.float32), 0.0)
    f = jnp.einsum('gts,bnsgd->bntgd', w, vp) + b_s.astype(jnp.float32).T[None, None, :, :, None]
    return f.reshape(B, nC * G_CHUNK, G_W)[:, :L]


def peer(h, w_pq, sub_keys, u_tab, v_tab):
    B, L, D = h.shape
    N = B * L
    nb = -(-N // P_BLOCK)
    t = jnp.pad(h.reshape(N, D), ((0, nb * P_BLOCK - N), (0, 0))).reshape(nb, P_BLOCK, D)

    def block(tb):
        q = (tb @ w_pq).reshape(P_BLOCK, P_HEADS, 2, P_HALF).astype(jnp.float32)
        s = jnp.einsum('thpd,hpkd->thpk', q, sub_keys.astype(jnp.float32))
        s_top, i_top = lax.top_k(s, P_TOPK)
        cand = (s_top[:, :, 0, :, None] + s_top[:, :, 1, None, :]).reshape(P_BLOCK, P_HEADS, P_TOPK * P_TOPK)
        cidx = (i_top[:, :, 0, :, None] * P_NKEYS + i_top[:, :, 1, None, :]).reshape(P_BLOCK, P_HEADS, P_TOPK * P_TOPK)
        best, pos = lax.top_k(cand, P_TOPK)
        eidx = jnp.take_along_axis(cidx, pos, -1)
        g = jax.nn.softmax(best, axis=-1)
        act = jax.nn.gelu(jnp.einsum('thkd,td->thk', u_tab[eidx], tb).astype(jnp.float32), approximate=False)
        return jnp.einsum('thk,thkd->td', (g * act).astype(v_tab.dtype), v_tab[eidx])

    out = lax.map(block, t)
    return out.reshape(nb * P_BLOCK, D)[:N].reshape(B, L, D)


def trunk_layer(x, c, conv_buf, C0, n0, m0, w_ada, b_ada, g_mix, w_in, conv_w, conv_b, wq_m, wk_m,
                b_i, b_f, mnorm_g, ln_g, ln_b, w_s, b_s, w_out, g_ffn, w_pq, sub_keys, u_tab, v_tab):
    f32 = jnp.float32
    B, L, D = x.shape
    mod = jax.nn.silu(c) @ w_ada + b_ada
    shift1, scale1, gate1, shift2, scale2, gate2 = [m_[:, None, :] for m_ in jnp.split(mod, N_MOD, axis=-1)]

    h = (rms_norm(x, g_mix) * (1 + scale1) + shift1).astype(x.dtype)
    z = h @ w_in
    qk_src, v_raw, o_pre, i_pre, f_pre, zu, zv, ga, gb = split_in(z)

    xpad = jnp.concatenate([conv_buf.astype(qk_src.dtype), qk_src], axis=1)
    conv_new = xpad[:, -(CONV_W - 1):]
    xc = jax.nn.silu(causal_conv(xpad, conv_w, conv_b, L)).reshape(B, L, M_HEADS, M_QK).astype(f32)
    q = jnp.einsum('blhd,hde->blhe', xc, wq_m.astype(f32))
    k = jnp.einsum('blhd,hde->blhe', xc, wk_m.astype(f32)) * (M_QK ** -0.5)
    v = v_raw.reshape(B, L, M_HEADS, M_V).astype(f32)
    ig = i_pre.astype(f32) + b_i.astype(f32)
    lf = jax.nn.log_sigmoid(f_pre.astype(f32) + b_f.astype(f32))
    hm, C1, n1, m1 = mlstm_scan(q, k, v, ig, lf, C0.astype(f32), n0.astype(f32), m0.astype(f32))
    hm = hm * lax.rsqrt(jnp.mean(hm * hm, -1, keepdims=True) + EPS) * mnorm_g.astype(f32).reshape(M_HEADS, M_V)
    hm = hm.reshape(B, L, M_V_W) * jax.nn.sigmoid(o_pre.astype(f32))

    u = jax.nn.gelu(zu.astype(f32), approximate=False)
    vn = layer_norm(jax.nn.gelu(zv.astype(f32), approximate=False), ln_g, ln_b)
    sg = u * spatial_mix(vn, w_s, b_s)

    merged = jax.nn.sigmoid(ga.astype(f32)) * hm + jax.nn.sigmoid(gb.astype(f32)) * sg
    x = x + (gate1 * (merged.astype(x.dtype) @ w_out)).astype(x.dtype)

    h2 = (rms_norm(x, g_ffn) * (1 + scale2) + shift2).astype(x.dtype)
    x = x + (gate2 * peer(h2, w_pq, sub_keys, u_tab, v_tab)).astype(x.dtype)
    return x, conv_new, C1, n1, m1, vn


def setup_inputs(seed: int = 0) -> dict:
    key = jax.random.key(seed)
    ks = iter(jax.random.split(key, 40))
    f32 = jnp.float32

    def nrm(shape, scale):
        return jax.random.normal(next(ks), shape, f32) * scale

    return {
        "x_prompt": nrm((BATCH, SEQ, D_MODEL), 1.0),
        "x_sample": nrm((DEC_BATCH, DEC_SEQ, D_MODEL), 1.0),
        "state_C": nrm((DEPTH, DEC_BATCH, M_HEADS, M_QK, M_V), 0.1),
        "state_n": nrm((DEPTH, DEC_BATCH, M_HEADS, M_QK), 0.1),
        "state_m": nrm((DEPTH, DEC_BATCH, M_HEADS), 1.0),
        "state_conv": nrm((DEPTH, DEC_BATCH, CONV_W - 1, M_QK_W), 1.0),
        "c_prompt": nrm((BATCH, D_MODEL), 1.0),
        "c_sample": nrm((DEC_BATCH, D_MODEL), 1.0),
        "w_ada": nrm((DEPTH, D_MODEL, N_MOD * D_MODEL), D_MODEL ** -0.5),
        "b_ada": nrm((DEPTH, N_MOD * D_MODEL), 0.01),
        "g_mix": 1.0 + nrm((DEPTH, D_MODEL), 0.01),
        "w_in": nrm((DEPTH, D_MODEL, IN_W), D_MODEL ** -0.5),
        "conv_w": nrm((DEPTH, CONV_W, M_QK_W), CONV_W ** -0.5),
        "conv_b": nrm((DEPTH, M_QK_W), 0.01),
        "wq_m": nrm((DEPTH, M_HEADS, M_QK, M_QK), M_QK ** -0.5),
        "wk_m": nrm((DEPTH, M_HEADS, M_QK, M_QK), M_QK ** -0.5),
        "b_i": nrm((DEPTH, M_HEADS), 0.1),
        "b_f": 3.0 + nrm((DEPTH, M_HEADS), 0.1),
        "mnorm_g": 1.0 + nrm((DEPTH, M_V_W), 0.01),
        "ln_g": 1.0 + nrm((DEPTH, G_W), 0.01),
        "ln_b": nrm((DEPTH, G_W), 0.01),
        "w_s": nrm((DEPTH, G_GROUPS, G_CHUNK, G_CHUNK), G_CHUNK ** -0.5),
        "b_s": 1.0 + nrm((DEPTH, G_GROUPS, G_CHUNK), 0.01),
        "w_out": nrm((DEPTH, D_MODEL, D_MODEL), D_MODEL ** -0.5),
        "g_ffn": 1.0 + nrm((DEPTH, D_MODEL), 0.01),
        "w_pq": nrm((DEPTH, D_MODEL, P_HEADS * P_QDIM), D_MODEL ** -0.5),
        "sub_keys": nrm((DEPTH, P_HEADS, 2, P_NKEYS, P_HALF), P_HALF ** -0.5),
        "u_tab": nrm((DEPTH, P_EXPERTS, D_MODEL), D_MODEL ** -0.5),
        "v_tab": nrm((DEPTH, P_EXPERTS, D_MODEL), P_HEADS ** -0.5),
        "g_final": 1.0 + nrm((D_MODEL,), 0.01),
    }


def reference(x_prompt, x_sample, state_C, state_n, state_m, state_conv, c_prompt, c_sample,
              w_ada, b_ada, g_mix, w_in, conv_w, conv_b, wq_m, wk_m, b_i, b_f, mnorm_g, ln_g, ln_b,
              w_s, b_s, w_out, g_ffn, w_pq, sub_keys, u_tab, v_tab, g_final):
    f32 = jnp.float32
    Bp = x_prompt.shape[0]
    xp, xs = x_prompt, x_sample
    pC, pn, pm, pconv = [], [], [], []
    sC, sn, sm, sconv, sv = [], [], [], [], []
    for l in range(DEPTH):
        lw = (w_ada[l], b_ada[l], g_mix[l], w_in[l], conv_w[l], conv_b[l], wq_m[l], wk_m[l], b_i[l], b_f[l],
              mnorm_g[l], ln_g[l], ln_b[l], w_s[l], b_s[l], w_out[l], g_ffn[l], w_pq[l], sub_keys[l],
              u_tab[l], v_tab[l])
        xp, cv, C1, n1, m1, _ = trunk_layer(
            xp, c_prompt, jnp.zeros((Bp, CONV_W - 1, M_QK_W), xp.dtype),
            jnp.zeros((Bp, M_HEADS, M_QK, M_V), f32), jnp.zeros((Bp, M_HEADS, M_QK), f32),
            jnp.zeros((Bp, M_HEADS), f32), *lw)
        pC.append(C1); pn.append(n1); pm.append(m1); pconv.append(cv)
        xs, cv, C1, n1, m1, vn = trunk_layer(
            xs, c_sample, state_conv[l], state_C[l], state_n[l], state_m[l], *lw)
        sC.append(C1); sn.append(n1); sm.append(m1); sconv.append(cv); sv.append(vn)
    y_prompt = rms_norm(xp, g_final)
    y_sample = rms_norm(xs, g_final)
    return (y_prompt, y_sample,
            jnp.stack(pC), jnp.stack(pn), jnp.stack(pm), jnp.stack(pconv),
            jnp.stack(sC), jnp.stack(sn), jnp.stack(sm), jnp.stack(sconv), jnp.stack(sv))
```

```python
import functools

import jax
import jax.numpy as jnp
from jax import lax
from jax.experimental import pallas as pl
from jax.experimental.pallas import tpu as pltpu

F32 = jnp.float32
BF16 = jnp.bfloat16
EPS = 1e-6
NEG_BIG = -1e30
LANES = 128
SUBLANES = 8
VMEM_LIMIT = 56 * 1024 * 1024
CONV_W = 4
P_TOPK = 16
N_MOD = 6
SQRT_HALF = 0.7071067811865476


def _cparams(n_axes):
    return pltpu.CompilerParams(dimension_semantics=("arbitrary",) * n_axes,
                                vmem_limit_bytes=VMEM_LIMIT)


def _gelu(x):
    return 0.5 * x * (1.0 + lax.erf(x * SQRT_HALF))


def _sigmoid(x):
    return 1.0 / (1.0 + jnp.exp(-x))


def _log_sigmoid(x):
    return jnp.minimum(x, 0.0) - jnp.log(1.0 + jnp.exp(-jnp.abs(x)))


def _dot(a, b):
    return jnp.dot(a, b, preferred_element_type=F32)


def _dot_nt(a, b):
    return lax.dot_general(a, b, (((1,), (1,)), ((), ())), preferred_element_type=F32)


def _dot_tn(a, b):
    return lax.dot_general(a, b, (((0,), (0,)), ((), ())), preferred_element_type=F32)


def _split3(a):
    a1 = a.astype(BF16)
    r1 = a - a1.astype(F32)
    a2 = r1.astype(BF16)
    a3 = (r1 - a2.astype(F32)).astype(BF16)
    return a1, a2, a3


def _adaln_body(c_ref, w_ref, b_ref, o_ref):
    c = c_ref[...]
    a = c * _sigmoid(c)
    a1 = a.astype(BF16)
    a2 = (a - a1.astype(F32)).astype(BF16)
    w = w_ref[...]
    w1 = w.astype(BF16)
    w2 = (w - w1.astype(F32)).astype(BF16)
    o_ref[...] = _dot(a1, w1) + _dot(a2, w1) + _dot(a1, w2) + b_ref[...]


def _adaln(c_all, w_ada, b_ada3, layer, tn):
    m, d = c_all.shape
    n = w_ada.shape[2]
    return pl.pallas_call(
        _adaln_body,
        grid=(n // tn,),
        in_specs=[pl.BlockSpec((m, d), lambda j: (0, 0)),
                  pl.BlockSpec((None, d, tn), lambda j: (layer, 0, j)),
                  pl.BlockSpec((None, 1, tn), lambda j: (layer, 0, j))],
        out_specs=pl.BlockSpec((m, tn), lambda j: (0, j)),
        out_shape=jax.ShapeDtypeStruct((m, n), F32),
        compiler_params=_cparams(1),
        name="adaln",
    )(c_all, w_ada, b_ada3)


def _row_tiling(b, l, tm):
    if l >= tm:
        assert l % tm == 0
        per = l // tm
        return 1, tm, b * per, (lambda i: (i // per, i % per))
    assert tm % l == 0 and b % (tm // l) == 0
    bb = tm // l
    return bb, l, b // bb, (lambda i: (i, 0))


def _modnorm_body(has_small, emit_h, tm, x_ref, g_ref, sc_ref, sh_ref, w_ref, *rest):
    rest = list(rest)
    ws_ref = rest.pop(0) if has_small else None
    o_ref = rest.pop(0)
    os_ref = rest.pop(0) if has_small else None
    ho_ref = rest.pop(0) if emit_h else None
    h_scr = rest.pop(0)

    @pl.when(pl.program_id(1) == 0)
    def _():
        bb, r, d = x_ref.shape
        nsub = 4
        while bb > 1 and bb % nsub:
            nsub //= 2
        rows = tm // nsub

        def sub(k, carry):
            if bb == 1:
                sl = (slice(None), pl.ds(pl.multiple_of(k * rows, rows), rows), slice(None))
                sc, sh = sc_ref[...], sh_ref[...]
            else:
                bsub = pl.ds(pl.multiple_of(k * (bb // nsub), bb // nsub), bb // nsub)
                sl = (bsub, slice(None), slice(None))
                sc, sh = sc_ref[bsub], sh_ref[bsub]
            x = x_ref[sl]
            var = jnp.mean(x * x, axis=-1, keepdims=True)
            y = x * lax.rsqrt(var + EPS) * g_ref[...]
            h = y * (1.0 + sc) + sh
            h_scr[pl.ds(pl.multiple_of(k * rows, rows), rows), :] = h.reshape(rows, d).astype(BF16)
            return carry

        lax.fori_loop(0, nsub, sub, 0)
        if has_small:
            os_ref[...] = _dot(h_scr[...], ws_ref[...])
        if emit_h:
            ho_ref[...] = h_scr[...]

    o_ref[...] = _dot(h_scr[...], w_ref[...])


def _modnorm_matmul(x3, g3, mod4, k_scale, k_shift, w, layer, tm, tn, w_small=None, emit_h=False):
    b, l, d = x3.shape
    nout = w.shape[1]
    bb, r, nt, ridx = _row_tiling(b, l, tm)
    has_small = w_small is not None

    def xmap(i, j):
        bi, ri = ridx(i)
        return (bi, ri, 0)

    def modmap(k):
        def f(i, j):
            return (ridx(i)[0], k, 0, 0)
        return f

    in_specs = [pl.BlockSpec((bb, r, d), xmap),
                pl.BlockSpec((None, 1, d), lambda i, j: (layer, 0, 0)),
                pl.BlockSpec((bb, None, 1, d), modmap(k_scale)),
                pl.BlockSpec((bb, None, 1, d), modmap(k_shift)),
                pl.BlockSpec((d, tn), lambda i, j: (0, j))]
    args = [x3, g3, mod4, mod4, w]
    out_specs = [pl.BlockSpec((tm, tn), lambda i, j: (i, j))]
    out_shape = [jax.ShapeDtypeStruct((b * l, nout), F32)]
    if has_small:
        ns = w_small.shape[1]
        in_specs.append(pl.BlockSpec((d, ns), lambda i, j: (0, 0)))
        args.append(w_small)
        out_specs.append(pl.BlockSpec((tm, ns), lambda i, j: (i, 0)))
        out_shape.append(jax.ShapeDtypeStruct((b * l, ns), F32))
    if emit_h:
        out_specs.append(pl.BlockSpec((tm, d), lambda i, j: (i, 0)))
        out_shape.append(jax.ShapeDtypeStruct((b * l, d), BF16))
    return pl.pallas_call(
        functools.partial(_modnorm_body, has_small, emit_h, tm),
        grid=(nt, nout // tn),
        in_specs=in_specs,
        out_specs=out_specs,
        out_shape=out_shape,
        scratch_shapes=[pltpu.VMEM((tm, d), BF16)],
        compiler_params=_cparams(2),
        name="modnorm_matmul",
    )(*args)


def _mlstm_body(ct, cs, l_real, heads, dk, dv,
                qk_ref, v_ref, o_ref, gt_ref, cb_ref, cw_ref, cbias_ref, wq_ref, wk_ref,
                bi_ref, bf_ref, mg_ref, c0_ref, n0_ref, m0_ref,
                hm_ref, c_ref, n_ref, m_ref, xbuf):
    hist = SUBLANES

    @pl.when(pl.program_id(1) == 0)
    def _():
        c_ref[...] = c0_ref[...]
        n_ref[...] = n0_ref[...]
        m_ref[...] = jnp.broadcast_to(m0_ref[...], m_ref.shape)
        xbuf[0:hist, :] = cb_ref[...]

    xbuf[hist:hist + ct, :] = qk_ref[...]
    acc = jnp.broadcast_to(cbias_ref[...], (ct, heads * dk))
    for j in range(CONV_W):
        acc = acc + xbuf[pl.ds(hist - (CONV_W - 1) + j, ct), :] * cw_ref[j:j + 1, :]
    xc = (acc * _sigmoid(acc)).astype(BF16)
    xbuf[0:hist, :] = xbuf[ct:ct + hist, :]

    g = gt_ref[...]
    ig_all = g[0:heads] + bi_ref[...]
    lf_all = _log_sigmoid(g[heads:2 * heads] + bf_ref[...])
    if l_real < cs:
        valid = lax.broadcasted_iota(jnp.int32, (heads, cs), 1) < l_real
        ig_all = jnp.where(valid, ig_all, NEG_BIG)
        lf_all = jnp.where(valid, lf_all, 0.0)
    upper = (lax.broadcasted_iota(jnp.int32, (cs, cs), 0)
             <= lax.broadcasted_iota(jnp.int32, (cs, cs), 1)).astype(BF16)
    l1, l2, l3 = _split3(lf_all)
    f_all = _dot(l1, upper) + _dot(l2, upper) + _dot(l3, upper)

    tt = lax.broadcasted_iota(jnp.int32, (ct, cs), 0)
    ss = lax.broadcasted_iota(jnp.int32, (ct, cs), 1)
    diag = tt == ss
    tril = ss <= tt
    pad_rows = cs - ct

    for h in range(heads):
        ig_r = ig_all[h:h + 1, :]
        f_r = f_all[h:h + 1, :]
        m0 = m_ref[h:h + 1, 0:1]
        f_c = jnp.sum(jnp.where(diag, f_r, 0.0), axis=1, keepdims=True)
        ig_c = jnp.sum(jnp.where(diag, ig_r, 0.0), axis=1, keepdims=True)
        cm_c = jnp.max(jnp.where(tril, ig_r - f_r, -jnp.inf), axis=1, keepdims=True)
        m_c = f_c + jnp.maximum(m0, cm_c)
        dmat = jnp.exp(jnp.where(tril, f_c - f_r + ig_r - m_c, -jnp.inf))
        a_c = jnp.exp(f_c + m0 - m_c)

        xh = xc[:, h * dk:(h + 1) * dk]
        q = _dot(xh, wq_ref[h])
        k = _dot(xh, wk_ref[h]) * (dk ** -0.5)
        v = v_ref[:, h * dv:(h + 1) * dv]
        m_l = m_c[ct - 1:ct, :]
        f_l = f_c[ct - 1:ct, :]
        ws_c = jnp.exp(f_l - f_c + ig_c - m_l)
        a_l = jnp.exp(f_l + m0 - m_l)
        kw = k * ws_c
        if pad_rows:
            k_p = jnp.concatenate([k, jnp.zeros((pad_rows, dk), F32)], axis=0)
            kw_p = jnp.concatenate([kw, jnp.zeros((pad_rows, dk), F32)], axis=0)
            v_p = jnp.concatenate([v, jnp.zeros((pad_rows, dv), F32)], axis=0)
        else:
            k_p, kw_p, v_p = k, kw, v
        q_b = q.astype(BF16)
        v_b = v_p.astype(BF16)
        c_old = c_ref[h]
        n_old = n_ref[h:h + 1, :]
        s = _dot_nt(q_b, k_p.astype(BF16)) * dmat
        num = a_c * _dot(q_b, c_old.astype(BF16)) + _dot(s.astype(BF16), v_b)
        dotn = (a_c * jnp.sum(q * n_old, axis=1, keepdims=True)
                + jnp.sum(s, axis=1, keepdims=True))
        den = jnp.maximum(jnp.abs(dotn), jnp.exp(-m_c)) + EPS
        hh = num / den

        c_ref[h] = a_l * c_old + _dot_tn(kw_p.astype(BF16), v_b)
        n_ref[h:h + 1, :] = a_l * n_old + jnp.sum(kw, axis=0, keepdims=True)
        m_ref[h:h + 1, :] = jnp.broadcast_to(m_l, (1, m_ref.shape[1]))

        hn = hh * lax.rsqrt(jnp.mean(hh * hh, axis=1, keepdims=True) + EPS)
        hn = hn * mg_ref[:, h * dv:(h + 1) * dv]
        hm_ref[:, h * dv:(h + 1) * dv] = hn * _sigmoid(o_ref[:, h * dv:(h + 1) * dv])


def _mlstm(z3, cols, gt, conv_hist, conv_w, conv_b3, wq, wk, bi3, bf3, mg3, c0, n0, m0,
           layer, state_layer, ct, cs, l_real):
    b, l, _ = z3.shape
    heads, dk, _ = wq.shape
    dv = c0.shape[-1]
    qkw, vw = heads * dk, heads * dv
    nc = l // ct
    assert cols["qk"] % qkw == 0 and cols["v"] % vw == 0 and cols["o"] % vw == 0
    body = functools.partial(_mlstm_body, ct, cs, l_real, heads, dk, dv)
    in_specs = [
        pl.BlockSpec((None, ct, qkw), lambda i, c: (i, c, cols["qk"] // qkw)),
        pl.BlockSpec((None, ct, vw), lambda i, c: (i, c, cols["v"] // vw)),
        pl.BlockSpec((None, ct, vw), lambda i, c: (i, c, cols["o"] // vw)),
        pl.BlockSpec((None, 2 * heads, cs), lambda i, c: (i, 0, c)),
        pl.BlockSpec((None, SUBLANES, qkw), lambda i, c: (i, 0, 0)),
        pl.BlockSpec((None, CONV_W, qkw), lambda i, c: (layer, 0, 0)),
        pl.BlockSpec((None, 1, qkw), lambda i, c: (layer, 0, 0)),
        pl.BlockSpec((heads, dk, dk), lambda i, c: (0, 0, 0)),
        pl.BlockSpec((heads, dk, dk), lambda i, c: (0, 0, 0)),
        pl.BlockSpec((None, heads, 1), lambda i, c: (layer, 0, 0)),
        pl.BlockSpec((None, heads, 1), lambda i, c: (layer, 0, 0)),
        pl.BlockSpec((None, 1, vw), lambda i, c: (layer, 0, 0)),
        pl.BlockSpec((None, None, heads, dk, dv), lambda i, c: (state_layer, i, 0, 0, 0)),
        pl.BlockSpec((None, None, heads, dk), lambda i, c: (state_layer, i, 0, 0)),
        pl.BlockSpec((None, None, heads, 1), lambda i, c: (state_layer, i, 0, 0)),
    ]
    out_specs = [
        pl.BlockSpec((None, ct, vw), lambda i, c: (i, c, 0)),
        pl.BlockSpec((None, heads, dk, dv), lambda i, c: (i, 0, 0, 0)),
        pl.BlockSpec((None, heads, dk), lambda i, c: (i, 0, 0)),
        pl.BlockSpec((None, heads, LANES), lambda i, c: (i, 0, 0)),
    ]
    out_shape = [
        jax.ShapeDtypeStruct((b, l, vw), F32),
        jax.ShapeDtypeStruct((b, heads, dk, dv), F32),
        jax.ShapeDtypeStruct((b, heads, dk), F32),
        jax.ShapeDtypeStruct((b, heads, LANES), F32),
    ]
    return pl.pallas_call(
        body,
        grid=(b, nc),
        in_specs=in_specs,
        out_specs=out_specs,
        out_shape=out_shape,
        scratch_shapes=[pltpu.VMEM((ct + SUBLANES, qkw), F32)],
        compiler_params=_cparams(2),
        name="mlstm",
    )(z3, z3, z3, gt, conv_hist, conv_w, conv_b3, wq, wk, bi3, bf3, mg3, c0, n0, m0)


def _gmlp_body(ct, cs, groups, gd, emit_vn, zu_ref, zv_ref, ga_ref, gb_ref, hm_ref, lng_ref, lnb_ref,
               ws_ref, bst_ref, *outs):
    mg_ref = outs[0]
    gv = _gelu(zv_ref[...])
    mu = jnp.mean(gv, axis=-1, keepdims=True)
    xc = gv - mu
    var = jnp.mean(xc * xc, axis=-1, keepdims=True)
    vn = xc * lax.rsqrt(var + EPS) * lng_ref[...] + lnb_ref[...]
    if emit_vn:
        outs[1][...] = vn
    vn_b = vn.astype(BF16)
    if cs > ct:
        vn_b = jnp.concatenate([vn_b.astype(F32), jnp.zeros((cs - ct, vn.shape[1]), F32)], axis=0).astype(BF16)
    tril = (lax.broadcasted_iota(jnp.int32, (ct, cs), 1) <= lax.broadcasted_iota(jnp.int32, (ct, cs), 0))
    for g in range(groups):
        sl = slice(g * gd, (g + 1) * gd)
        w = jnp.where(tril, ws_ref[g, 0:ct, :], 0.0).astype(BF16)
        f = _dot(w, vn_b[:, sl]) + bst_ref[0:ct, g:g + 1]
        sg = _gelu(zu_ref[:, sl]) * f
        merged = _sigmoid(ga_ref[:, sl]) * hm_ref[:, sl] + _sigmoid(gb_ref[:, sl]) * sg
        mg_ref[:, sl] = merged.astype(mg_ref.dtype)


def _gmlp(z3, cols, hm, lng3, lnb3, w_s, b_st, layer, ct, cs, emit_vn, merged_dtype):
    b, l, _ = z3.shape
    gw = hm.shape[-1]
    groups = w_s.shape[1]
    gd = gw // groups
    nc = l // ct

    def zspec(name):
        assert cols[name] % gw == 0
        return pl.BlockSpec((None, ct, gw), lambda i, c: (i, c, cols[name] // gw))

    in_specs = [zspec("zu"), zspec("zv"), zspec("ga"), zspec("gb"),
                pl.BlockSpec((None, ct, gw), lambda i, c: (i, c, 0)),
                pl.BlockSpec((None, 1, gw), lambda i, c: (layer, 0, 0)),
                pl.BlockSpec((None, 1, gw), lambda i, c: (layer, 0, 0)),
                pl.BlockSpec((None, groups, cs, cs), lambda i, c: (layer, 0, 0, 0)),
                pl.BlockSpec((None, cs, groups), lambda i, c: (layer, 0, 0))]
    out_specs = [pl.BlockSpec((None, ct, gw), lambda i, c: (i, c, 0))]
    out_shape = [jax.ShapeDtypeStruct((b, l, gw), merged_dtype)]
    if emit_vn:
        out_specs.append(pl.BlockSpec((None, ct, gw), lambda i, c: (i, c, 0)))
        out_shape.append(jax.ShapeDtypeStruct((b, l, gw), F32))
    return pl.pallas_call(
        functools.partial(_gmlp_body, ct, cs, groups, gd, emit_vn),
        grid=(b, nc),
        in_specs=in_specs,
        out_specs=out_specs,
        out_shape=out_shape,
        compiler_params=_cparams(2),
        name="gmlp_merge",
    )(z3, z3, z3, z3, hm, lng3, lnb3, w_s, b_st)


def _outproj_body(a_ref, w_ref, x_ref, gate_ref, o_ref):
    acc = _dot(a_ref[...].astype(BF16), w_ref[...])
    o_ref[...] = x_ref[...] + gate_ref[...] * acc.reshape(x_ref.shape)


def _outproj(merged2, w, x3, mod4, k_gate, tm, tn):
    b, l, d = x3.shape
    bb, r, nt, ridx = _row_tiling(b, l, tm)
    kdim = merged2.shape[1]
    return pl.pallas_call(
        _outproj_body,
        grid=(nt, d // tn),
        in_specs=[pl.BlockSpec((tm, kdim), lambda i, j: (i, 0)),
                  pl.BlockSpec((kdim, tn), lambda i, j: (0, j)),
                  pl.BlockSpec((bb, r, tn), lambda i, j: ridx(i) + (j,)),
                  pl.BlockSpec((bb, None, 1, tn), lambda i, j: (ridx(i)[0], k_gate, 0, j))],
        out_specs=pl.BlockSpec((bb, r, tn), lambda i, j: ridx(i) + (j,)),
        out_shape=jax.ShapeDtypeStruct((b, l, d), F32),
        compiler_params=_cparams(2),
        name="outproj_residual",
    )(merged2, w, x3, mod4)


def _peer_pairs():
    return [(a, b) for a in range(P_TOPK) for b in range(P_TOPK) if (a + 1) * (b + 1) <= P_TOPK]


def _topk_body(nk, tt, pq_ref, keys_ref, st_ref, aux_ref, topv, topi, cand, cid):
    p = pl.program_id(1) % 2
    riota = lax.broadcasted_iota(jnp.int32, (nk, tt), 0).astype(F32)
    s = _dot_nt(keys_ref[...].astype(BF16), pq_ref[...].astype(BF16))
    st_ref[...] = s
    x = s
    for a in range(P_TOPK):
        m = jnp.max(x, axis=0, keepdims=True)
        idx = jnp.min(jnp.where(x == m, riota, float(nk)), axis=0, keepdims=True)
        topv[p, a:a + 1, :] = m
        topi[p, a:a + 1, :] = idx
        x = jnp.where(riota == idx, -jnp.inf, x)

    @pl.when(p == 1)
    def _():
        pairs = _peer_pairs()
        npad = cand.shape[0]
        cand[len(pairs):npad, :] = jnp.full((npad - len(pairs), tt), -jnp.inf, F32)
        cid[len(pairs):npad, :] = jnp.full((npad - len(pairs), tt), 0.0, F32)
        for r, (a, b) in enumerate(pairs):
            cand[r:r + 1, :] = topv[0, a:a + 1, :] + topv[1, b:b + 1, :]
            cid[r:r + 1, :] = topi[0, a:a + 1, :] * float(nk) + topi[1, b:b + 1, :]
        x = cand[...]
        ids = cid[...]
        big = float(nk * nk)
        mx = None
        z = None
        for a in range(P_TOPK):
            m = jnp.max(x, axis=0, keepdims=True)
            e = jnp.min(jnp.where(x == m, ids, big), axis=0, keepdims=True)
            if a == 0:
                mx = m
                z = jnp.ones_like(m)
            else:
                z = z + jnp.exp(m - mx)
            x = jnp.where(ids == e, -jnp.inf, x)
        aux_ref[0:1, :] = m
        aux_ref[1:2, :] = e
        aux_ref[2:3, :] = mx
        aux_ref[3:4, :] = 1.0 / z
        aux_ref[4:8, :] = jnp.zeros((4, tt), F32)


def _peer_topk(pq, keys4, layer, tt):
    n, qw = pq.shape
    hp, nk, half = keys4.shape[1:]
    heads = hp // 2
    npairs = -(-len(_peer_pairs()) // SUBLANES) * SUBLANES
    return pl.pallas_call(
        functools.partial(_topk_body, nk, tt),
        grid=(n // tt, hp),
        in_specs=[pl.BlockSpec((tt, half), lambda i, k: (i, k)),
                  pl.BlockSpec((None, None, nk, half), lambda i, k: (layer, k, 0, 0))],
        out_specs=[pl.BlockSpec((None, nk, tt), lambda i, k: (k, 0, i)),
                   pl.BlockSpec((None, SUBLANES, tt), lambda i, k: (k // 2, 0, i))],
        out_shape=[jax.ShapeDtypeStruct((hp, nk, n), F32),
                   jax.ShapeDtypeStruct((heads, SUBLANES, n), F32)],
        scratch_shapes=[pltpu.VMEM((2, P_TOPK, tt), F32), pltpu.VMEM((2, P_TOPK, tt), F32),
                        pltpu.VMEM((npairs, tt), F32), pltpu.VMEM((npairs, tt), F32)],
        compiler_params=_cparams(2),
        name="peer_topk",
    )(pq, keys4)


def _peer_body(heads, nk, tt, ec, h2_ref, st_ref, aux_ref, u_ref, v_ref, o_ref, at_scr, p_scr):
    j = pl.program_id(1)
    at_scr[...] = _dot_nt(u_ref[...], h2_ref[...])
    rows = lax.broadcasted_iota(jnp.int32, (nk, LANES), 0)
    per = ec // nk
    grp = SUBLANES // per
    base = pl.multiple_of((j // grp) * SUBLANES, SUBLANES)
    sub = j % grp
    for ii in range(per):
        i_glob = j * per + ii
        eio = (i_glob * nk + rows).astype(F32)
        for lb in range(tt // LANES):
            ln = slice(lb * LANES, (lb + 1) * LANES)
            wacc = jnp.zeros((nk, LANES), F32)
            for h in range(heads):
                tile = st_ref[2 * h, pl.ds(base, SUBLANES), ln]
                s0b = tile[ii:ii + 1]
                for k in range(1, grp):
                    s0b = jnp.where(sub == k, tile[k * per + ii:k * per + ii + 1], s0b)
                cnd = st_ref[2 * h + 1, :, ln] + s0b
                thr = aux_ref[h, 0:1, ln]
                est = aux_ref[h, 1:2, ln]
                mx = aux_ref[h, 2:3, ln]
                rz = aux_ref[h, 3:4, ln]
                sel = (cnd > thr) | ((cnd == thr) & (eio <= est))
                wacc = wacc + jnp.where(sel, jnp.exp(cnd - mx) * rz, 0.0)
            pt = wacc * _gelu(at_scr[ii * nk:(ii + 1) * nk, ln])
            p_scr[ln, ii * nk:(ii + 1) * nk] = pt.T.astype(BF16)
    @pl.when(j == 0)
    def _():
        o_ref[...] = jnp.zeros(o_ref.shape, F32)

    d = o_ref.shape[1]
    cw = min(d, 1024)
    for cb in range(d // cw):
        o_ref[:, cb * cw:(cb + 1) * cw] += _dot(p_scr[...], v_ref[:, cb * cw:(cb + 1) * cw])


def _peer_dense(h2, st, aux, u_tab, v_tab, tt, ec):
    n, d = h2.shape
    e = u_tab.shape[0]
    hp, nk, _ = st.shape
    heads = hp // 2
    assert nk == LANES and e == nk * nk and ec % nk == 0 and tt % LANES == 0
    assert SUBLANES % (ec // nk) == 0
    return pl.pallas_call(
        functools.partial(_peer_body, heads, nk, tt, ec),
        grid=(n // tt, e // ec),
        in_specs=[pl.BlockSpec((tt, d), lambda i, j: (i, 0)),
                  pl.BlockSpec((hp, nk, tt), lambda i, j: (0, 0, i)),
                  pl.BlockSpec((heads, SUBLANES, tt), lambda i, j: (0, 0, i)),
                  pl.BlockSpec((ec, d), lambda i, j: (j, 0)),
                  pl.BlockSpec((ec, d), lambda i, j: (j, 0))],
        out_specs=pl.BlockSpec((tt, d), lambda i, j: (i, 0)),
        out_shape=jax.ShapeDtypeStruct((n, d), F32),
        scratch_shapes=[pltpu.VMEM((ec, tt), F32), pltpu.VMEM((tt, ec), BF16)],
        compiler_params=_cparams(2),
        name="peer_dense",
    )(h2, st, aux, u_tab, v_tab)


def _resid_body(final, x_ref, p_ref, gate_ref, *rest):
    x = x_ref[...] + gate_ref[...] * p_ref[...]
    if final:
        gf_ref, o_ref = rest
        o_ref[...] = x * lax.rsqrt(jnp.mean(x * x, axis=-1, keepdims=True) + EPS) * gf_ref[...]
    else:
        rest[0][...] = x


def _residual(x3, p3, mod4, k_gate, tm, g_final2=None):
    b, l, d = x3.shape
    bb, r, nt, ridx = _row_tiling(b, l, tm)
    final = g_final2 is not None
    in_specs = [pl.BlockSpec((bb, r, d), lambda i: ridx(i) + (0,)),
                pl.BlockSpec((bb, r, d), lambda i: ridx(i) + (0,)),
                pl.BlockSpec((bb, None, 1, d), lambda i: (ridx(i)[0], k_gate, 0, 0))]
    args = [x3, p3, mod4]
    if final:
        in_specs.append(pl.BlockSpec((1, d), lambda i: (0, 0)))
        args.append(g_final2)
    return pl.pallas_call(
        functools.partial(_resid_body, final),
        grid=(nt,),
        in_specs=in_specs,
        out_specs=pl.BlockSpec((bb, r, d), lambda i: ridx(i) + (0,)),
        out_shape=jax.ShapeDtypeStruct((b, l, d), F32),
        compiler_params=_cparams(1),
        name="residual",
    )(*args)


def _pick(n, pref):
    t = min(n, pref)
    while n % t:
        t //= 2
    return t


def _trunk_layer(x3, mod4, l_real, conv_hist, c0, n0, m0, state_layer, lw, layer, g_final2, want_vn):
    b, l, d = x3.shape
    n = b * l
    heads, dk, _ = lw["wq"].shape
    qkw = heads * dk
    cols = lw["cols"]
    chunk = LANES
    ct = min(chunk, l)
    tm = _pick(n, 512)

    z, gates = _modnorm_matmul(x3, lw["g_mix"], mod4, 1, 0, lw["w_in"], layer, tm,
                               _pick(lw["w_in"].shape[1], 1024), w_small=lw["w_gate"])
    z3 = z.reshape(b, l, -1)
    gt = jnp.swapaxes(gates[:, :2 * heads].reshape(b, l, 2 * heads), 1, 2)
    if l % chunk:
        gt = jnp.pad(gt, ((0, 0), (0, 0), (0, chunk - l % chunk)))
    conv_new = z3[:, l_real - (CONV_W - 1):l_real, cols["qk"]:cols["qk"] + qkw]
    hm, c1, n1, m1 = _mlstm(z3, cols, gt, conv_hist, lw["conv_w"], lw["conv_b"], lw["wq"], lw["wk"],
                            lw["b_i"], lw["b_f"], lw["mnorm_g"], c0, n0, m0,
                            layer, state_layer, ct, chunk, l_real)
    outs = _gmlp(z3, cols, hm, lw["ln_g"], lw["ln_b"], lw["w_s"], lw["b_st"], layer, ct, chunk,
                 want_vn, BF16 if ct % 16 == 0 else F32)
    merged = outs[0]
    vn = outs[1] if want_vn else None
    x3 = _outproj(merged.reshape(n, -1), lw["w_out"], x3, mod4, 2, tm, _pick(d, 1024))

    pq, h2 = _modnorm_matmul(x3, lw["g_ffn"], mod4, 4, 3, lw["w_pq"], layer, tm,
                             _pick(lw["w_pq"].shape[1], 512), emit_h=True)
    tt = _pick(n, 512)
    st, aux = _peer_topk(pq, lw["keys"], layer, tt)
    peer = _peer_dense(h2, st, aux, lw["u_tab"], lw["v_tab"], tt, _pick(lw["u_tab"].shape[0], 512))
    x3 = _residual(x3, peer.reshape(b, l, d), mod4, 5, _pick(n, 256), g_final2)
    return x3, conv_new, c1, n1, m1[:, :, 0], vn


def kernel(x_prompt, x_sample, state_C, state_n, state_m, state_conv, c_prompt, c_sample, w_ada, b_ada, g_mix, w_in, conv_w, conv_b, wq_m, wk_m, b_i, b_f, mnorm_g, ln_g, ln_b, w_s, b_s, w_out, g_ffn, w_pq, sub_keys, u_tab, v_tab, g_final):
    depth = w_ada.shape[0]
    bp, lp, d = x_prompt.shape
    bs, ls, _ = x_sample.shape
    heads, dk = wq_m.shape[1], wq_m.shape[2]
    dv = state_C.shape[-1]
    qkw, vw = heads * dk, heads * dv
    gw = ln_g.shape[1]

    ls_pad = -(-ls // SUBLANES) * SUBLANES
    xs = jnp.pad(x_sample, ((0, 0), (0, ls_pad - ls), (0, 0)))
    xp = x_prompt

    n_c = bp + bs
    n_c_pad = -(-n_c // SUBLANES) * SUBLANES
    c_all = jnp.pad(jnp.concatenate([c_prompt, c_sample], axis=0), ((0, n_c_pad - n_c), (0, 0)))

    o_v, o_o, o_g = qkw, qkw + vw, qkw + 2 * vw
    o_zu = o_g + 2 * heads
    cols = {"v": 0, "o": vw, "zu": 2 * vw, "zv": 2 * vw + gw, "ga": 2 * vw + 2 * gw,
            "gb": 2 * vw + 2 * gw + d, "qk": 2 * vw + 2 * gw + 2 * d}

    hist_pad = SUBLANES - (CONV_W - 1)
    zero_hist = jnp.zeros((bp, SUBLANES, qkw), F32)
    zero_c = jnp.zeros((1, bp, heads, dk, dv), F32)
    zero_n = jnp.zeros((1, bp, heads, dk), F32)
    zero_m = jnp.zeros((1, bp, heads, 1), F32)
    state_m4 = state_m[..., None]
    g_final2 = g_final.reshape(1, d)
    b_ada3 = b_ada[:, None, :]

    p_out = [[], [], [], []]
    s_out = [[], [], [], [], []]
    for l in range(depth):
        wl = w_in[l]
        lw = {
            "cols": cols,
            "g_mix": g_mix[:, None, :], "g_ffn": g_ffn[:, None, :],
            "w_in": jnp.concatenate([wl[:, o_v:o_g], wl[:, o_zu:], wl[:, :qkw]], axis=1).astype(BF16),
            "w_gate": jnp.pad(wl[:, o_g:o_zu], ((0, 0), (0, LANES - 2 * heads))).astype(BF16),
            "conv_w": conv_w, "conv_b": conv_b[:, None, :],
            "wq": wq_m[l].astype(BF16), "wk": wk_m[l].astype(BF16),
            "b_i": b_i[:, :, None], "b_f": b_f[:, :, None],
            "mnorm_g": mnorm_g[:, None, :], "ln_g": ln_g[:, None, :], "ln_b": ln_b[:, None, :],
            "w_s": w_s, "b_st": jnp.swapaxes(b_s, 1, 2),
            "w_out": w_out[l].astype(BF16), "w_pq": w_pq[l].astype(BF16),
            "keys": sub_keys.reshape(depth, -1, sub_keys.shape[3], sub_keys.shape[4]),
            "u_tab": u_tab[l].astype(BF16), "v_tab": v_tab[l].astype(BF16),
        }
        mod = _adaln(c_all, w_ada, b_ada3, l, _pick(w_ada.shape[2], 512))
        mod_p = mod[:bp].reshape(bp, N_MOD, 1, d)
        mod_s = mod[bp:n_c].reshape(bs, N_MOD, 1, d)
        last = l == depth - 1

        xp, cv, c1, n1, m1, _ = _trunk_layer(xp, mod_p, lp, zero_hist, zero_c, zero_n, zero_m, 0,
                                             lw, l, g_final2 if last else None, False)
        for acc, val in zip(p_out, (c1, n1, m1, cv)):
            acc.append(val)

        hist_s = jnp.pad(state_conv[l], ((0, 0), (hist_pad, 0), (0, 0)))
        xs, cv, c1, n1, m1, vn = _trunk_layer(xs, mod_s, ls, hist_s, state_C, state_n, state_m4, l,
                                              lw, l, g_final2 if last else None, True)
        for acc, val in zip(s_out, (c1, n1, m1, cv, vn[:, :ls])):
            acc.append(val)

    return (xp, xs[:, :ls]) + tuple(jnp.stack(a) for a in p_out) + tuple(jnp.stack(a) for a in s_out)
```

```python
import functools

import jax
import jax.numpy as jnp
from jax import lax
from jax.experimental import pallas as pl
from jax.experimental.pallas import tpu as pltpu

F32 = jnp.float32
BF16 = jnp.bfloat16
EPS = 1e-6
NEG_BIG = -1e30
LANES = 128
SUBLANES = 8
VMEM_LIMIT = 56 * 1024 * 1024
CONV_W = 4
P_TOPK = 16
N_MOD = 6
SQRT_HALF = 0.7071067811865476


def _cparams(n_axes):
    return pltpu.CompilerParams(dimension_semantics=("arbitrary",) * n_axes,
                                vmem_limit_bytes=VMEM_LIMIT)


def _gelu(x):
    return 0.5 * x * (1.0 + lax.erf(x * SQRT_HALF))


def _sigmoid(x):
    return 1.0 / (1.0 + jnp.exp(-x))


def _log_sigmoid(x):
    return jnp.minimum(x, 0.0) - jnp.log(1.0 + jnp.exp(-jnp.abs(x)))


def _dot(a, b):
    return jnp.dot(a, b, preferred_element_type=F32)


def _dot_nt(a, b):
    return lax.dot_general(a, b, (((1,), (1,)), ((), ())), preferred_element_type=F32)


def _dot_tn(a, b):
    return lax.dot_general(a, b, (((0,), (0,)), ((), ())), preferred_element_type=F32)


def _split3(a):
    a1 = a.astype(BF16)
    r1 = a - a1.astype(F32)
    a2 = r1.astype(BF16)
    a3 = (r1 - a2.astype(F32)).astype(BF16)
    return a1, a2, a3


def _adaln_body(c_ref, w_ref, b_ref, o_ref):
    c = c_ref[...]
    a = c * _sigmoid(c)
    a1 = a.astype(BF16)
    a2 = (a - a1.astype(F32)).astype(BF16)
    w1 = w_ref[...].astype(BF16)
    o_ref[...] = _dot(a1, w1) + _dot(a2, w1) + b_ref[...]


def _adaln(c_all, w_ada, b_ada3, layer, tn):
    m, d = c_all.shape
    n = w_ada.shape[2]
    return pl.pallas_call(
        _adaln_body,
        grid=(n // tn,),
        in_specs=[pl.BlockSpec((m, d), lambda j: (0, 0)),
                  pl.BlockSpec((None, d, tn), lambda j: (layer, 0, j)),
                  pl.BlockSpec((None, 1, tn), lambda j: (layer, 0, j))],
        out_specs=pl.BlockSpec((m, tn), lambda j: (0, j)),
        out_shape=jax.ShapeDtypeStruct((m, n), F32),
        compiler_params=_cparams(1),
        name="adaln",
    )(c_all, w_ada, b_ada3)


def _row_tiling(b, l, tm):
    if l >= tm:
        assert l % tm == 0
        per = l // tm
        return 1, tm, b * per, (lambda i: (i // per, i % per))
    assert tm % l == 0 and b % (tm // l) == 0
    bb = tm // l
    return bb, l, b // bb, (lambda i: (i, 0))


def _modnorm_body(has_small, emit_h, tm, x_ref, g_ref, sc_ref, sh_ref, w_ref, *rest):
    rest = list(rest)
    ws_ref = rest.pop(0) if has_small else None
    o_ref = rest.pop(0)
    os_ref = rest.pop(0) if has_small else None
    ho_ref = rest.pop(0) if emit_h else None
    h_scr = rest.pop(0)

    @pl.when(pl.program_id(1) == 0)
    def _():
        bb, r, d = x_ref.shape
        nsub = 4
        while bb > 1 and bb % nsub:
            nsub //= 2
        rows = tm // nsub

        def sub(k, carry):
            if bb == 1:
                sl = (slice(None), pl.ds(pl.multiple_of(k * rows, rows), rows), slice(None))
                sc, sh = sc_ref[...], sh_ref[...]
            else:
                bsub = pl.ds(pl.multiple_of(k * (bb // nsub), bb // nsub), bb // nsub)
                sl = (bsub, slice(None), slice(None))
                sc, sh = sc_ref[bsub], sh_ref[bsub]
            x = x_ref[sl]
            var = jnp.mean(x * x, axis=-1, keepdims=True)
            y = x * lax.rsqrt(var + EPS) * g_ref[...]
            h = y * (1.0 + sc) + sh
            h_scr[pl.ds(pl.multiple_of(k * rows, rows), rows), :] = h.reshape(rows, d).astype(BF16)
            return carry

        lax.fori_loop(0, nsub, sub, 0)
        if has_small:
            os_ref[...] = _dot(h_scr[...], ws_ref[...])
        if emit_h:
            ho_ref[...] = h_scr[...]

    o_ref[...] = _dot(h_scr[...], w_ref[...])


def _modnorm_matmul(x3, g3, mod4, k_scale, k_shift, w, layer, tm, tn, w_small=None, emit_h=False):
    b, l, d = x3.shape
    nout = w.shape[1]
    bb, r, nt, ridx = _row_tiling(b, l, tm)
    has_small = w_small is not None

    def xmap(i, j):
        bi, ri = ridx(i)
        return (bi, ri, 0)

    def modmap(k):
        def f(i, j):
            return (ridx(i)[0], k, 0, 0)
        return f

    in_specs = [pl.BlockSpec((bb, r, d), xmap),
                pl.BlockSpec((None, 1, d), lambda i, j: (layer, 0, 0)),
                pl.BlockSpec((bb, None, 1, d), modmap(k_scale)),
                pl.BlockSpec((bb, None, 1, d), modmap(k_shift)),
                pl.BlockSpec((d, tn), lambda i, j: (0, j))]
    args = [x3, g3, mod4, mod4, w]
    out_specs = [pl.BlockSpec((tm, tn), lambda i, j: (i, j))]
    out_shape = [jax.ShapeDtypeStruct((b * l, nout), F32)]
    if has_small:
        ns = w_small.shape[1]
        in_specs.append(pl.BlockSpec((d, ns), lambda i, j: (0, 0)))
        args.append(w_small)
        out_specs.append(pl.BlockSpec((tm, ns), lambda i, j: (i, 0)))
        out_shape.append(jax.ShapeDtypeStruct((b * l, ns), F32))
    if emit_h:
        out_specs.append(pl.BlockSpec((tm, d), lambda i, j: (i, 0)))
        out_shape.append(jax.ShapeDtypeStruct((b * l, d), BF16))
    return pl.pallas_call(
        functools.partial(_modnorm_body, has_small, emit_h, tm),
        grid=(nt, nout // tn),
        in_specs=in_specs,
        out_specs=out_specs,
        out_shape=out_shape,
        scratch_shapes=[pltpu.VMEM((tm, d), BF16)],
        compiler_params=_cparams(2),
        name="modnorm_matmul",
    )(*args)


def _mlstm_body(ct, cs, l_real, heads, dk, dv, stacked,
                qk_ref, v_ref, o_ref, gt_ref, cb_ref, cw_ref, cbias_ref, wq_ref, wk_ref,
                bi_ref, bf_ref, mg_ref, c0_ref, n0_ref, m0_ref, *rest):
    if stacked:
        prev_ref, hm_ref, call_ref, n_ref, m_ref, xbuf = rest
        c_ref = call_ref.at[1]
    else:
        hm_ref, c_ref, n_ref, m_ref, xbuf = rest
    hist = SUBLANES

    @pl.when(pl.program_id(1) == 0)
    def _():
        if stacked:
            call_ref[0] = prev_ref[...]
        c_ref[...] = c0_ref[...]
        n_ref[...] = n0_ref[...]
        m_ref[...] = jnp.broadcast_to(m0_ref[...], m_ref.shape)
        xbuf[0:hist, :] = cb_ref[...]

    xbuf[hist:hist + ct, :] = qk_ref[...]
    acc = jnp.broadcast_to(cbias_ref[...], (ct, heads * dk))
    for j in range(CONV_W):
        acc = acc + xbuf[pl.ds(hist - (CONV_W - 1) + j, ct), :] * cw_ref[j:j + 1, :]
    xc = (acc * _sigmoid(acc)).astype(BF16)
    xbuf[0:hist, :] = xbuf[ct:ct + hist, :]

    g = gt_ref[...]
    ig_all = g[0:heads] + bi_ref[...]
    lf_all = _log_sigmoid(g[heads:2 * heads] + bf_ref[...])
    if l_real < cs:
        valid = lax.broadcasted_iota(jnp.int32, (heads, cs), 1) < l_real
        ig_all = jnp.where(valid, ig_all, NEG_BIG)
        lf_all = jnp.where(valid, lf_all, 0.0)
    upper = (lax.broadcasted_iota(jnp.int32, (cs, cs), 0)
             <= lax.broadcasted_iota(jnp.int32, (cs, cs), 1)).astype(BF16)
    l1, l2, l3 = _split3(lf_all)
    f_all = _dot(l1, upper) + _dot(l2, upper) + _dot(l3, upper)

    tt = lax.broadcasted_iota(jnp.int32, (ct, cs), 0)
    ss = lax.broadcasted_iota(jnp.int32, (ct, cs), 1)
    diag = tt == ss
    tril = ss <= tt
    pad_rows = cs - ct

    for h in range(heads):
        ig_r = ig_all[h:h + 1, :]
        f_r = f_all[h:h + 1, :]
        m0 = m_ref[h:h + 1, 0:1]
        f_c = jnp.sum(jnp.where(diag, f_r, 0.0), axis=1, keepdims=True)
        ig_c = jnp.sum(jnp.where(diag, ig_r, 0.0), axis=1, keepdims=True)
        cm_c = jnp.max(jnp.where(tril, ig_r - f_r, -jnp.inf), axis=1, keepdims=True)
        m_c = f_c + jnp.maximum(m0, cm_c)
        dmat = jnp.exp(jnp.where(tril, f_c - f_r + ig_r - m_c, -jnp.inf))
        a_c = jnp.exp(f_c + m0 - m_c)

        xh = xc[:, h * dk:(h + 1) * dk]
        q = _dot(xh, wq_ref[h])
        k = _dot(xh, wk_ref[h]) * (dk ** -0.5)
        v = v_ref[:, h * dv:(h + 1) * dv]
        m_l = m_c[ct - 1:ct, :]
        f_l = f_c[ct - 1:ct, :]
        ws_c = jnp.exp(f_l - f_c + ig_c - m_l)
        a_l = jnp.exp(f_l + m0 - m_l)
        kw = k * ws_c
        if pad_rows:
            k_p = jnp.concatenate([k, jnp.zeros((pad_rows, dk), F32)], axis=0)
            kw_p = jnp.concatenate([kw, jnp.zeros((pad_rows, dk), F32)], axis=0)
            v_p = jnp.concatenate([v, jnp.zeros((pad_rows, dv), F32)], axis=0)
        else:
            k_p, kw_p, v_p = k, kw, v
        q_b = q.astype(BF16)
        v_b = v_p.astype(BF16)
        c_old = c_ref[h]
        n_old = n_ref[h:h + 1, :]
        s = _dot_nt(q_b, k_p.astype(BF16)) * dmat
        num = a_c * _dot(q_b, c_old.astype(BF16)) + _dot(s.astype(BF16), v_b)
        dotn = (a_c * jnp.sum(q * n_old, axis=1, keepdims=True)
                + jnp.sum(s, axis=1, keepdims=True))
        den = jnp.maximum(jnp.abs(dotn), jnp.exp(-m_c)) + EPS
        hh = num / den

        c_ref[h] = a_l * c_old + _dot_tn(kw_p.astype(BF16), v_b)
        n_ref[h:h + 1, :] = a_l * n_old + jnp.sum(kw, axis=0, keepdims=True)
        m_ref[h:h + 1, :] = jnp.broadcast_to(m_l, (1, m_ref.shape[1]))

        hn = hh * lax.rsqrt(jnp.mean(hh * hh, axis=1, keepdims=True) + EPS)
        hn = hn * mg_ref[:, h * dv:(h + 1) * dv]
        hm_ref[:, h * dv:(h + 1) * dv] = hn * _sigmoid(o_ref[:, h * dv:(h + 1) * dv])


def _mlstm(z3, cols, gt, conv_hist, conv_w, conv_b3, wq, wk, bi3, bf3, mg3, c0, n0, m0,
           layer, state_layer, ct, cs, l_real, prev_c=None):
    b, l, _ = z3.shape
    heads, dk, _ = wq.shape
    dv = c0.shape[-1]
    qkw, vw = heads * dk, heads * dv
    nc = l // ct
    stacked = prev_c is not None
    assert cols["qk"] % qkw == 0 and cols["v"] % vw == 0 and cols["o"] % vw == 0
    body = functools.partial(_mlstm_body, ct, cs, l_real, heads, dk, dv, stacked)
    in_specs = [
        pl.BlockSpec((None, ct, qkw), lambda i, c: (i, c, cols["qk"] // qkw)),
        pl.BlockSpec((None, ct, vw), lambda i, c: (i, c, cols["v"] // vw)),
        pl.BlockSpec((None, ct, vw), lambda i, c: (i, c, cols["o"] // vw)),
        pl.BlockSpec((None, 2 * heads, cs), lambda i, c: (i, 0, c)),
        pl.BlockSpec((None, SUBLANES, qkw), lambda i, c: (i, 0, 0)),
        pl.BlockSpec((None, CONV_W, qkw), lambda i, c: (layer, 0, 0)),
        pl.BlockSpec((None, 1, qkw), lambda i, c: (layer, 0, 0)),
        pl.BlockSpec((heads, dk, dk), lambda i, c: (0, 0, 0)),
        pl.BlockSpec((heads, dk, dk), lambda i, c: (0, 0, 0)),
        pl.BlockSpec((None, heads, 1), lambda i, c: (layer, 0, 0)),
        pl.BlockSpec((None, heads, 1), lambda i, c: (layer, 0, 0)),
        pl.BlockSpec((None, 1, vw), lambda i, c: (layer, 0, 0)),
        pl.BlockSpec((None, None, heads, dk, dv), lambda i, c: (state_layer, i, 0, 0, 0)),
        pl.BlockSpec((None, None, heads, dk), lambda i, c: (state_layer, i, 0, 0)),
        pl.BlockSpec((None, None, heads, 1), lambda i, c: (state_layer, i, 0, 0)),
    ]
    out_specs = [
        pl.BlockSpec((None, ct, vw), lambda i, c: (i, c, 0)),
        pl.BlockSpec((None, heads, dk, dv), lambda i, c: (i, 0, 0, 0)),
        pl.BlockSpec((None, heads, dk), lambda i, c: (i, 0, 0)),
        pl.BlockSpec((None, heads, LANES), lambda i, c: (i, 0, 0)),
    ]
    out_shape = [
        jax.ShapeDtypeStruct((b, l, vw), F32),
        jax.ShapeDtypeStruct((b, heads, dk, dv), F32),
        jax.ShapeDtypeStruct((b, heads, dk), F32),
        jax.ShapeDtypeStruct((b, heads, LANES), F32),
    ]
    args = [z3, z3, z3, gt, conv_hist, conv_w, conv_b3, wq, wk, bi3, bf3, mg3, c0, n0, m0]
    if stacked:
        in_specs.append(pl.BlockSpec((None, heads, dk, dv), lambda i, c: (i, 0, 0, 0)))
        args.append(prev_c)
        out_specs[1] = pl.BlockSpec((2, None, heads, dk, dv), lambda i, c: (0, i, 0, 0, 0))
        out_shape[1] = jax.ShapeDtypeStruct((2, b, heads, dk, dv), F32)
    return pl.pallas_call(
        body,
        grid=(b, nc),
        in_specs=in_specs,
        out_specs=out_specs,
        out_shape=out_shape,
        scratch_shapes=[pltpu.VMEM((ct + SUBLANES, qkw), F32)],
        compiler_params=_cparams(2),
        name="mlstm",
    )(*args)


def _gmlp_body(ct, cs, groups, gd, emit_vn, zu_ref, zv_ref, ga_ref, gb_ref, hm_ref, lng_ref, lnb_ref,
               ws_ref, bst_ref, *outs):
    mg_ref = outs[0]
    gv = _gelu(zv_ref[...])
    mu = jnp.mean(gv, axis=-1, keepdims=True)
    xc = gv - mu
    var = jnp.mean(xc * xc, axis=-1, keepdims=True)
    vn = xc * lax.rsqrt(var + EPS) * lng_ref[...] + lnb_ref[...]
    if emit_vn:
        outs[1][...] = vn
    vn_b = vn.astype(BF16)
    if cs > ct:
        vn_b = jnp.concatenate([vn_b.astype(F32), jnp.zeros((cs - ct, vn.shape[1]), F32)], axis=0).astype(BF16)
    tril = (lax.broadcasted_iota(jnp.int32, (ct, cs), 1) <= lax.broadcasted_iota(jnp.int32, (ct, cs), 0))
    for g in range(groups):
        sl = slice(g * gd, (g + 1) * gd)
        w = jnp.where(tril, ws_ref[g, 0:ct, :], 0.0).astype(BF16)
        f = _dot(w, vn_b[:, sl]) + bst_ref[0:ct, g:g + 1]
        sg = _gelu(zu_ref[:, sl]) * f
        merged = _sigmoid(ga_ref[:, sl]) * hm_ref[:, sl] + _sigmoid(gb_ref[:, sl]) * sg
        mg_ref[:, sl] = merged.astype(mg_ref.dtype)


def _gmlp(z3, cols, hm, lng3, lnb3, w_s, b_st, layer, ct, cs, emit_vn, merged_dtype):
    b, l, _ = z3.shape
    gw = hm.shape[-1]
    groups = w_s.shape[1]
    gd = gw // groups
    nc = l // ct

    def zspec(name):
        assert cols[name] % gw == 0
        return pl.BlockSpec((None, ct, gw), lambda i, c: (i, c, cols[name] // gw))

    in_specs = [zspec("zu"), zspec("zv"), zspec("ga"), zspec("gb"),
                pl.BlockSpec((None, ct, gw), lambda i, c: (i, c, 0)),
                pl.BlockSpec((None, 1, gw), lambda i, c: (layer, 0, 0)),
                pl.BlockSpec((None, 1, gw), lambda i, c: (layer, 0, 0)),
                pl.BlockSpec((None, groups, cs, cs), lambda i, c: (layer, 0, 0, 0)),
                pl.BlockSpec((None, cs, groups), lambda i, c: (layer, 0, 0))]
    out_specs = [pl.BlockSpec((None, ct, gw), lambda i, c: (i, c, 0))]
    out_shape = [jax.ShapeDtypeStruct((b, l, gw), merged_dtype)]
    if emit_vn:
        out_specs.append(pl.BlockSpec((None, ct, gw), lambda i, c: (i, c, 0)))
        out_shape.append(jax.ShapeDtypeStruct((b, l, gw), F32))
    return pl.pallas_call(
        functools.partial(_gmlp_body, ct, cs, groups, gd, emit_vn),
        grid=(b, nc),
        in_specs=in_specs,
        out_specs=out_specs,
        out_shape=out_shape,
        compiler_params=_cparams(2),
        name="gmlp_merge",
    )(z3, z3, z3, z3, hm, lng3, lnb3, w_s, b_st)


def _outproj_body(a_ref, w_ref, x_ref, gate_ref, o_ref):
    acc = _dot(a_ref[...].astype(BF16), w_ref[...])
    o_ref[...] = x_ref[...] + gate_ref[...] * acc.reshape(x_ref.shape)


def _outproj(merged2, w, x3, mod4, k_gate, tm, tn):
    b, l, d = x3.shape
    bb, r, nt, ridx = _row_tiling(b, l, tm)
    kdim = merged2.shape[1]
    return pl.pallas_call(
        _outproj_body,
        grid=(nt, d // tn),
        in_specs=[pl.BlockSpec((tm, kdim), lambda i, j: (i, 0)),
                  pl.BlockSpec((kdim, tn), lambda i, j: (0, j)),
                  pl.BlockSpec((bb, r, tn), lambda i, j: ridx(i) + (j,)),
                  pl.BlockSpec((bb, None, 1, tn), lambda i, j: (ridx(i)[0], k_gate, 0, j))],
        out_specs=pl.BlockSpec((bb, r, tn), lambda i, j: ridx(i) + (j,)),
        out_shape=jax.ShapeDtypeStruct((b, l, d), F32),
        compiler_params=_cparams(2),
        name="outproj_residual",
    )(merged2, w, x3, mod4)


def _peer_pairs():
    k = P_TOPK + 1
    return [(a, b) for a in range(k) for b in range(k) if (a + 1) * (b + 1) <= k]


def _topk_body(nk, tt, pq_ref, keys_ref, st_ref, aux_ref, topv, topi, cand, cid):
    p = pl.program_id(1) % 2
    riota = lax.broadcasted_iota(jnp.int32, (nk, tt), 0).astype(F32)
    s = _dot_nt(keys_ref[...].astype(BF16), pq_ref[...].astype(BF16))
    st_ref[...] = s
    x = s
    for a in range(P_TOPK + 1):
        m = jnp.max(x, axis=0, keepdims=True)
        idx = jnp.min(jnp.where(x == m, riota, float(nk)), axis=0, keepdims=True)
        topv[p, a:a + 1, :] = m
        topi[p, a:a + 1, :] = idx
        x = jnp.where(riota == idx, -jnp.inf, x)

    @pl.when(p == 1)
    def _():
        pairs = _peer_pairs()
        npad = cand.shape[0]
        cand[len(pairs):npad, :] = jnp.full((npad - len(pairs), tt), -jnp.inf, F32)
        cid[len(pairs):npad, :] = jnp.full((npad - len(pairs), tt), 0.0, F32)
        for r, (a, b) in enumerate(pairs):
            cand[r:r + 1, :] = topv[0, a:a + 1, :] + topv[1, b:b + 1, :]
            cid[r:r + 1, :] = topi[0, a:a + 1, :] * float(nk) + topi[1, b:b + 1, :]
        x = cand[...]
        ids = cid[...]
        big = float(nk * nk)
        mx = None
        z = None
        for a in range(P_TOPK):
            m = jnp.max(x, axis=0, keepdims=True)
            e = jnp.min(jnp.where(x == m, ids, big), axis=0, keepdims=True)
            if a == 0:
                mx = m
                z = jnp.ones_like(m)
            else:
                z = z + jnp.exp(m - mx)
            x = jnp.where(ids == e, -jnp.inf, x)
        runner_up = jnp.max(x, axis=0, keepdims=True)
        aux_ref[0:1, :] = m
        aux_ref[1:2, :] = e
        aux_ref[2:3, :] = mx + jnp.log(z)
        aux_ref[3:4, :] = jnp.where(runner_up == m, 1.0, 0.0)
        aux_ref[4:8, :] = jnp.zeros((4, tt), F32)


def _peer_topk(pq, keys4, layer, tt):
    n, qw = pq.shape
    hp, nk, half = keys4.shape[1:]
    heads = hp // 2
    npairs = -(-len(_peer_pairs()) // SUBLANES) * SUBLANES
    ntop = -(-(P_TOPK + 1) // SUBLANES) * SUBLANES
    return pl.pallas_call(
        functools.partial(_topk_body, nk, tt),
        grid=(n // tt, hp),
        in_specs=[pl.BlockSpec((tt, half), lambda i, k: (i, k)),
                  pl.BlockSpec((None, None, nk, half), lambda i, k: (layer, k, 0, 0))],
        out_specs=[pl.BlockSpec((None, nk, tt), lambda i, k: (k, 0, i)),
                   pl.BlockSpec((None, SUBLANES, tt), lambda i, k: (k // 2, 0, i))],
        out_shape=[jax.ShapeDtypeStruct((hp, nk, n), F32),
                   jax.ShapeDtypeStruct((heads, SUBLANES, n), F32)],
        scratch_shapes=[pltpu.VMEM((2, ntop, tt), F32), pltpu.VMEM((2, ntop, tt), F32),
                        pltpu.VMEM((npairs, tt), F32), pltpu.VMEM((npairs, tt), F32)],
        compiler_params=_cparams(2),
        name="peer_topk",
    )(pq, keys4)


def _peer_body(heads, nk, tt, ec, tie_ref, h2_ref, st_ref, aux_ref, u_ref, v_ref, o_ref, at_scr, w_scr, p_scr):
    j = pl.program_id(1)
    rt = 4 * SUBLANES
    rows = lax.broadcasted_iota(jnp.int32, (rt, LANES), 0)
    per = ec // nk
    grp = SUBLANES // per
    base = pl.multiple_of((j // grp) * SUBLANES, SUBLANES)
    sub = j % grp

    def weights(exact_ties):
        for ii in range(per):
            i_glob = j * per + ii
            for lb in range(tt // LANES):
                ln = slice(lb * LANES, (lb + 1) * LANES)
                for r in range(nk // rt):
                    wacc = jnp.zeros((rt, LANES), F32)
                    for h in range(heads):
                        tile = st_ref[2 * h, pl.ds(base, SUBLANES), ln]
                        s0b = tile[ii:ii + 1]
                        for k in range(1, grp):
                            s0b = jnp.where(sub == k, tile[k * per + ii:k * per + ii + 1], s0b)
                        cnd = st_ref[2 * h + 1, r * rt:(r + 1) * rt, ln] + s0b
                        thr = aux_ref[h, 0:1, ln]
                        c0 = aux_ref[h, 2:3, ln]
                        if exact_ties:
                            eio = (i_glob * nk + r * rt + rows).astype(F32)
                            sel = (cnd > thr) | ((cnd == thr) & (eio <= aux_ref[h, 1:2, ln]))
                        else:
                            sel = cnd >= thr
                        wacc = wacc + jnp.where(sel, jnp.exp(cnd - c0), 0.0)
                    w_scr[ii * nk + r * rt:ii * nk + (r + 1) * rt, ln] = wacc

    tie = tie_ref[pl.program_id(0)]

    @pl.when(tie == 0)
    def _():
        weights(False)

    @pl.when(tie != 0)
    def _():
        weights(True)

    at_scr[...] = _dot_nt(u_ref[...], h2_ref[...])
    for ii in range(per):
        for lb in range(tt // LANES):
            ln = slice(lb * LANES, (lb + 1) * LANES)
            pt = w_scr[ii * nk:(ii + 1) * nk, ln] * _gelu(at_scr[ii * nk:(ii + 1) * nk, ln])
            p_scr[ln, ii * nk:(ii + 1) * nk] = pt.T.astype(BF16)

    @pl.when(j == 0)
    def _():
        o_ref[...] = jnp.zeros(o_ref.shape, F32)

    d = o_ref.shape[1]
    cw = min(d, 1024)
    for cb in range(d // cw):
        o_ref[:, cb * cw:(cb + 1) * cw] += _dot(p_scr[...], v_ref[:, cb * cw:(cb + 1) * cw])


def _peer_dense(h2, st, aux, tie, u_tab, v_tab, tt, ec):
    n, d = h2.shape
    e = u_tab.shape[0]
    hp, nk, _ = st.shape
    heads = hp // 2
    assert nk == LANES and e == nk * nk and ec % nk == 0 and tt % LANES == 0
    assert SUBLANES % (ec // nk) == 0
    return pl.pallas_call(
        functools.partial(_peer_body, heads, nk, tt, ec),
        grid_spec=pltpu.PrefetchScalarGridSpec(
            num_scalar_prefetch=1,
            grid=(n // tt, e // ec),
            in_specs=[pl.BlockSpec((tt, d), lambda i, j, f: (i, 0)),
                      pl.BlockSpec((hp, nk, tt), lambda i, j, f: (0, 0, i)),
                      pl.BlockSpec((heads, SUBLANES, tt), lambda i, j, f: (0, 0, i)),
                      pl.BlockSpec((ec, d), lambda i, j, f: (j, 0)),
                      pl.BlockSpec((ec, d), lambda i, j, f: (j, 0))],
            out_specs=pl.BlockSpec((tt, d), lambda i, j, f: (i, 0)),
            scratch_shapes=[pltpu.VMEM((ec, tt), F32), pltpu.VMEM((ec, tt), F32), pltpu.VMEM((tt, ec), BF16)]),
        out_shape=jax.ShapeDtypeStruct((n, d), F32),
        compiler_params=_cparams(2),
        name="peer_dense",
    )(tie, h2, st, aux, u_tab, v_tab)


def _resid_body(final, x_ref, p_ref, gate_ref, *rest):
    x = x_ref[...] + gate_ref[...] * p_ref[...]
    if final:
        gf_ref, o_ref = rest
        o_ref[...] = x * lax.rsqrt(jnp.mean(x * x, axis=-1, keepdims=True) + EPS) * gf_ref[...]
    else:
        rest[0][...] = x


def _residual(x3, p3, mod4, k_gate, tm, g_final2=None):
    b, l, d = x3.shape
    bb, r, nt, ridx = _row_tiling(b, l, tm)
    final = g_final2 is not None
    in_specs = [pl.BlockSpec((bb, r, d), lambda i: ridx(i) + (0,)),
                pl.BlockSpec((bb, r, d), lambda i: ridx(i) + (0,)),
                pl.BlockSpec((bb, None, 1, d), lambda i: (ridx(i)[0], k_gate, 0, 0))]
    args = [x3, p3, mod4]
    if final:
        in_specs.append(pl.BlockSpec((1, d), lambda i: (0, 0)))
        args.append(g_final2)
    return pl.pallas_call(
        functools.partial(_resid_body, final),
        grid=(nt,),
        in_specs=in_specs,
        out_specs=pl.BlockSpec((bb, r, d), lambda i: ridx(i) + (0,)),
        out_shape=jax.ShapeDtypeStruct((b, l, d), F32),
        compiler_params=_cparams(1),
        name="residual",
    )(*args)


def _pick(n, pref):
    t = min(n, pref)
    while n % t:
        t //= 2
    return t


def _trunk_layer(x3, mod4, l_real, conv_hist, c0, n0, m0, state_layer, lw, layer, g_final2, want_vn,
                 prev_c=None):
    b, l, d = x3.shape
    n = b * l
    heads, dk, _ = lw["wq"].shape
    qkw = heads * dk
    cols = lw["cols"]
    chunk = LANES
    ct = min(chunk, l)
    tm = _pick(n, 512)

    z, gates = _modnorm_matmul(x3, lw["g_mix"], mod4, 1, 0, lw["w_in"], layer, tm,
                               _pick(lw["w_in"].shape[1], 1024), w_small=lw["w_gate"])
    z3 = z.reshape(b, l, -1)
    gt = jnp.swapaxes(gates[:, :2 * heads].reshape(b, l, 2 * heads), 1, 2)
    if l % chunk:
        gt = jnp.pad(gt, ((0, 0), (0, 0), (0, chunk - l % chunk)))
    conv_new = z3[:, l_real - (CONV_W - 1):l_real, cols["qk"]:cols["qk"] + qkw]
    hm, c1, n1, m1 = _mlstm(z3, cols, gt, conv_hist, lw["conv_w"], lw["conv_b"], lw["wq"], lw["wk"],
                            lw["b_i"], lw["b_f"], lw["mnorm_g"], c0, n0, m0,
                            layer, state_layer, ct, chunk, l_real, prev_c)
    outs = _gmlp(z3, cols, hm, lw["ln_g"], lw["ln_b"], lw["w_s"], lw["b_st"], layer, ct, chunk,
                 want_vn, BF16 if ct % 16 == 0 else F32)
    merged = outs[0]
    vn = outs[1] if want_vn else None
    x3 = _outproj(merged.reshape(n, -1), lw["w_out"], x3, mod4, 2, tm, _pick(d, 1024))

    pq, h2 = _modnorm_matmul(x3, lw["g_ffn"], mod4, 4, 3, lw["w_pq"], layer, tm,
                             _pick(lw["w_pq"].shape[1], 512), emit_h=True)
    if l_real < l:
        pq = pq.reshape(b, l, -1)[:, :l_real].reshape(b * l_real, -1)
        h2 = h2.reshape(b, l, -1)[:, :l_real].reshape(b * l_real, -1)
    nr = b * l_real
    tt = _pick(nr, 512)
    st, aux = _peer_topk(pq, lw["keys"], layer, tt)
    tie = (jnp.max(aux[:, 3, :].reshape(-1, nr // tt, tt), axis=(0, 2)) > 0).astype(jnp.int32)
    peer = _peer_dense(h2, st, aux, tie, lw["u_tab"], lw["v_tab"], tt, _pick(lw["u_tab"].shape[0], 512))
    peer = peer.reshape(b, l_real, d)
    if l_real < l:
        peer = jnp.pad(peer, ((0, 0), (0, l - l_real), (0, 0)))
    x3 = _residual(x3, peer, mod4, 5, _pick(n, 256), g_final2)
    return x3, conv_new, c1, n1, m1[:, :, 0], vn


def kernel(x_prompt, x_sample, state_C, state_n, state_m, state_conv, c_prompt, c_sample, w_ada, b_ada, g_mix, w_in, conv_w, conv_b, wq_m, wk_m, b_i, b_f, mnorm_g, ln_g, ln_b, w_s, b_s, w_out, g_ffn, w_pq, sub_keys, u_tab, v_tab, g_final):
    depth = w_ada.shape[0]
    bp, lp, d = x_prompt.shape
    bs, ls, _ = x_sample.shape
    heads, dk = wq_m.shape[1], wq_m.shape[2]
    dv = state_C.shape[-1]
    qkw, vw = heads * dk, heads * dv
    gw = ln_g.shape[1]

    ls_pad = -(-ls // SUBLANES) * SUBLANES
    xs = jnp.pad(x_sample, ((0, 0), (0, ls_pad - ls), (0, 0)))
    xp = x_prompt

    n_c = bp + bs
    n_c_pad = -(-n_c // SUBLANES) * SUBLANES
    c_all = jnp.pad(jnp.concatenate([c_prompt, c_sample], axis=0), ((0, n_c_pad - n_c), (0, 0)))

    o_v, o_o, o_g = qkw, qkw + vw, qkw + 2 * vw
    o_zu = o_g + 2 * heads
    cols = {"v": 0, "o": vw, "zu": 2 * vw, "zv": 2 * vw + gw, "ga": 2 * vw + 2 * gw,
            "gb": 2 * vw + 2 * gw + d, "qk": 2 * vw + 2 * gw + 2 * d}

    hist_pad = SUBLANES - (CONV_W - 1)
    zero_hist = jnp.zeros((bp, SUBLANES, qkw), F32)
    zero_c = jnp.zeros((1, bp, heads, dk, dv), F32)
    zero_n = jnp.zeros((1, bp, heads, dk), F32)
    zero_m = jnp.zeros((1, bp, heads, 1), F32)
    state_m4 = state_m[..., None]
    g_final2 = g_final.reshape(1, d)
    b_ada3 = b_ada[:, None, :]

    p_out = [[], [], [], []]
    s_out = [[], [], [], [], []]
    for l in range(depth):
        wl = w_in[l]
        lw = {
            "cols": cols,
            "g_mix": g_mix[:, None, :], "g_ffn": g_ffn[:, None, :],
            "w_in": jnp.concatenate([wl[:, o_v:o_g], wl[:, o_zu:], wl[:, :qkw]], axis=1).astype(BF16),
            "w_gate": jnp.pad(wl[:, o_g:o_zu], ((0, 0), (0, LANES - 2 * heads))).astype(BF16),
            "conv_w": conv_w, "conv_b": conv_b[:, None, :],
            "wq": wq_m[l].astype(BF16), "wk": wk_m[l].astype(BF16),
            "b_i": b_i[:, :, None], "b_f": b_f[:, :, None],
            "mnorm_g": mnorm_g[:, None, :], "ln_g": ln_g[:, None, :], "ln_b": ln_b[:, None, :],
            "w_s": w_s, "b_st": jnp.swapaxes(b_s, 1, 2),
            "w_out": w_out[l].astype(BF16), "w_pq": w_pq[l].astype(BF16),
            "keys": sub_keys.reshape(depth, -1, sub_keys.shape[3], sub_keys.shape[4]),
            "u_tab": u_tab[l].astype(BF16), "v_tab": v_tab[l].astype(BF16),
        }
        mod = _adaln(c_all, w_ada, b_ada3, l, _pick(w_ada.shape[2], 512))
        mod_p = mod[:bp].reshape(bp, N_MOD, 1, d)
        mod_s = mod[bp:n_c].reshape(bs, N_MOD, 1, d)
        last = l == depth - 1

        xp, cv, c1, n1, m1, _ = _trunk_layer(xp, mod_p, lp, zero_hist, zero_c, zero_n, zero_m, 0,
                                             lw, l, g_final2 if last else None, False)
        for acc, val in zip(p_out, (c1, n1, m1, cv)):
            acc.append(val)

        hist_s = jnp.pad(state_conv[l], ((0, 0), (hist_pad, 0), (0, 0)))
        stack_c = last and depth == 2
        xs, cv, c1, n1, m1, vn = _trunk_layer(xs, mod_s, ls, hist_s, state_C, state_n, state_m4, l,
                                              lw, l, g_final2 if last else None, True,
                                              s_out[0][0] if stack_c else None)
        if stack_c:
            sample_c = c1
        for acc, val in zip(s_out, (c1, n1, m1, cv, vn[:, :ls])):
            acc.append(val)

    s_stacked = [jnp.stack(a) for a in s_out[1:]]
    s_stacked.insert(0, sample_c if depth == 2 else jnp.stack(s_out[0]))
    return (xp, xs[:, :ls]) + tuple(jnp.stack(a) for a in p_out) + tuple(s_stacked)
```

```python
import functools

import jax
import jax.numpy as jnp
from jax import lax
from jax.experimental import pallas as pl
from jax.experimental.pallas import tpu as pltpu

F32 = jnp.float32
BF16 = jnp.bfloat16
EPS = 1e-6
NEG_BIG = -1e30
LANES = 128
SUBLANES = 8
VMEM_LIMIT = 56 * 1024 * 1024
CONV_W = 4
P_TOPK = 16
N_MOD = 6
SQRT_HALF = 0.7071067811865476


def _cparams(n_axes):
    return pltpu.CompilerParams(dimension_semantics=("arbitrary",) * n_axes,
                                vmem_limit_bytes=VMEM_LIMIT)


def _gelu(x):
    return 0.5 * x * (1.0 + lax.erf(x * SQRT_HALF))


def _sigmoid(x):
    return 1.0 / (1.0 + jnp.exp(-x))


def _log_sigmoid(x):
    return jnp.minimum(x, 0.0) - jnp.log(1.0 + jnp.exp(-jnp.abs(x)))


def _dot(a, b):
    return jnp.dot(a, b, preferred_element_type=F32)


def _dot_nt(a, b):
    return lax.dot_general(a, b, (((1,), (1,)), ((), ())), preferred_element_type=F32)


def _dot_tn(a, b):
    return lax.dot_general(a, b, (((0,), (0,)), ((), ())), preferred_element_type=F32)


def _split3(a):
    a1 = a.astype(BF16)
    r1 = a - a1.astype(F32)
    a2 = r1.astype(BF16)
    a3 = (r1 - a2.astype(F32)).astype(BF16)
    return a1, a2, a3


def _adaln_body(c_ref, w_ref, b_ref, o_ref):
    c = c_ref[...]
    a = c * _sigmoid(c)
    a1 = a.astype(BF16)
    a2 = (a - a1.astype(F32)).astype(BF16)
    w1 = w_ref[...].astype(BF16)
    o_ref[...] = _dot(a1, w1) + _dot(a2, w1) + b_ref[...]


def _adaln(c_all, w_ada, b_ada3, layer, tn):
    m, d = c_all.shape
    n = w_ada.shape[2]
    return pl.pallas_call(
        _adaln_body,
        grid=(n // tn,),
        in_specs=[pl.BlockSpec((m, d), lambda j: (0, 0)),
                  pl.BlockSpec((None, d, tn), lambda j: (layer, 0, j)),
                  pl.BlockSpec((None, 1, tn), lambda j: (layer, 0, j))],
        out_specs=pl.BlockSpec((m, tn), lambda j: (0, j)),
        out_shape=jax.ShapeDtypeStruct((m, n), F32),
        compiler_params=_cparams(1),
        name="adaln",
    )(c_all, w_ada, b_ada3)


def _row_tiling(b, l, tm):
    if l >= tm:
        assert l % tm == 0
        per = l // tm
        return 1, tm, b * per, (lambda i: (i // per, i % per))
    assert tm % l == 0 and b % (tm // l) == 0
    bb = tm // l
    return bb, l, b // bb, (lambda i: (i, 0))


def _modnorm_body(has_small, emit_h, tm, x_ref, g_ref, sc_ref, sh_ref, w_ref, *rest):
    rest = list(rest)
    ws_ref = rest.pop(0) if has_small else None
    o_ref = rest.pop(0)
    os_ref = rest.pop(0) if has_small else None
    ho_ref = rest.pop(0) if emit_h else None
    h_scr = rest.pop(0)

    @pl.when(pl.program_id(1) == 0)
    def _():
        bb, r, d = x_ref.shape
        nsub = 4
        while bb > 1 and bb % nsub:
            nsub //= 2
        rows = tm // nsub

        def sub(k, carry):
            if bb == 1:
                sl = (slice(None), pl.ds(pl.multiple_of(k * rows, rows), rows), slice(None))
                sc, sh = sc_ref[...], sh_ref[...]
            else:
                bsub = pl.ds(pl.multiple_of(k * (bb // nsub), bb // nsub), bb // nsub)
                sl = (bsub, slice(None), slice(None))
                sc, sh = sc_ref[bsub], sh_ref[bsub]
            x = x_ref[sl]
            var = jnp.mean(x * x, axis=-1, keepdims=True)
            y = x * lax.rsqrt(var + EPS) * g_ref[...]
            h = y * (1.0 + sc) + sh
            h_scr[pl.ds(pl.multiple_of(k * rows, rows), rows), :] = h.reshape(rows, d).astype(BF16)
            return carry

        lax.fori_loop(0, nsub, sub, 0)
        if has_small:
            os_ref[...] = _dot(h_scr[...], ws_ref[...])
        if emit_h:
            ho_ref[...] = h_scr[...]

    o_ref[...] = _dot(h_scr[...], w_ref[...])


def _modnorm_matmul(x3, g3, mod4, k_scale, k_shift, w, layer, tm, tn, w_small=None, emit_h=False):
    b, l, d = x3.shape
    nout = w.shape[2]
    bb, r, nt, ridx = _row_tiling(b, l, tm)
    has_small = w_small is not None

    def xmap(i, j):
        bi, ri = ridx(i)
        return (bi, ri, 0)

    def modmap(k):
        def f(i, j):
            return (ridx(i)[0], k, 0, 0)
        return f

    in_specs = [pl.BlockSpec((bb, r, d), xmap),
                pl.BlockSpec((None, 1, d), lambda i, j: (layer, 0, 0)),
                pl.BlockSpec((bb, None, 1, d), modmap(k_scale)),
                pl.BlockSpec((bb, None, 1, d), modmap(k_shift)),
                pl.BlockSpec((None, d, tn), lambda i, j: (layer, 0, j))]
    args = [x3, g3, mod4, mod4, w]
    out_specs = [pl.BlockSpec((tm, tn), lambda i, j: (i, j))]
    out_shape = [jax.ShapeDtypeStruct((b * l, nout), F32)]
    if has_small:
        ns = w_small.shape[2]
        in_specs.append(pl.BlockSpec((None, d, ns), lambda i, j: (layer, 0, 0)))
        args.append(w_small)
        out_specs.append(pl.BlockSpec((tm, ns), lambda i, j: (i, 0)))
        out_shape.append(jax.ShapeDtypeStruct((b * l, ns), F32))
    if emit_h:
        out_specs.append(pl.BlockSpec((tm, d), lambda i, j: (i, 0)))
        out_shape.append(jax.ShapeDtypeStruct((b * l, d), BF16))
    return pl.pallas_call(
        functools.partial(_modnorm_body, has_small, emit_h, tm),
        grid=(nt, nout // tn),
        in_specs=in_specs,
        out_specs=out_specs,
        out_shape=out_shape,
        scratch_shapes=[pltpu.VMEM((tm, d), BF16)],
        compiler_params=_cparams(2),
        name="modnorm_matmul",
    )(*args)


def _mlstm_body(ct, cs, l_real, heads, dk, dv, stacked,
                qk_ref, v_ref, o_ref, gt_ref, cb_ref, cw_ref, cbias_ref, wq_ref, wk_ref,
                bi_ref, bf_ref, mg_ref, c0_ref, n0_ref, m0_ref, *rest):
    if stacked:
        prev_ref, hm_ref, call_ref, n_ref, m_ref, xbuf = rest
        c_ref = call_ref.at[1]
    else:
        hm_ref, c_ref, n_ref, m_ref, xbuf = rest
    hist = SUBLANES

    @pl.when(pl.program_id(1) == 0)
    def _():
        if stacked:
            call_ref[0] = prev_ref[...]
        c_ref[...] = c0_ref[...]
        n_ref[...] = n0_ref[...]
        m_ref[...] = jnp.broadcast_to(m0_ref[...], m_ref.shape)
        xbuf[0:hist, :] = cb_ref[...]

    xbuf[hist:hist + ct, :] = qk_ref[...]
    acc = jnp.broadcast_to(cbias_ref[...], (ct, heads * dk))
    for j in range(CONV_W):
        acc = acc + xbuf[pl.ds(hist - (CONV_W - 1) + j, ct), :] * cw_ref[j:j + 1, :]
    xc = (acc * _sigmoid(acc)).astype(BF16)
    xbuf[0:hist, :] = xbuf[ct:ct + hist, :]

    g = gt_ref[...]
    ig_all = g[0:heads] + bi_ref[...]
    lf_all = _log_sigmoid(g[heads:2 * heads] + bf_ref[...])
    if l_real < cs:
        valid = lax.broadcasted_iota(jnp.int32, (heads, cs), 1) < l_real
        ig_all = jnp.where(valid, ig_all, NEG_BIG)
        lf_all = jnp.where(valid, lf_all, 0.0)
    upper = (lax.broadcasted_iota(jnp.int32, (cs, cs), 0)
             <= lax.broadcasted_iota(jnp.int32, (cs, cs), 1)).astype(BF16)
    l1, l2, l3 = _split3(lf_all)
    f_all = _dot(l1, upper) + _dot(l2, upper) + _dot(l3, upper)

    tt = lax.broadcasted_iota(jnp.int32, (ct, cs), 0)
    ss = lax.broadcasted_iota(jnp.int32, (ct, cs), 1)
    diag = tt == ss
    tril = ss <= tt
    pad_rows = cs - ct

    m_all = m_ref[...]
    n_all = n_ref[...]

    def gates_and_projections(h):
        ig_r = ig_all[h:h + 1, :]
        f_r = f_all[h:h + 1, :]
        m0 = m_all[h:h + 1, 0:1]
        f_c = jnp.sum(jnp.where(diag, f_r, 0.0), axis=1, keepdims=True)
        ig_c = jnp.sum(jnp.where(diag, ig_r, 0.0), axis=1, keepdims=True)
        cm_c = jnp.max(jnp.where(tril, ig_r - f_r, -jnp.inf), axis=1, keepdims=True)
        m_c = f_c + jnp.maximum(m0, cm_c)
        dmat = jnp.exp(jnp.where(tril, f_c - f_r + ig_r - m_c, -jnp.inf))
        a_c = jnp.exp(f_c + m0 - m_c)
        xh = xc[:, h * dk:(h + 1) * dk]
        q = _dot(xh, wq_ref[h])
        k = _dot(xh, wk_ref[h]) * (dk ** -0.5)
        v = v_ref[:, h * dv:(h + 1) * dv]
        m_l = m_c[ct - 1:ct, :]
        f_l = f_c[ct - 1:ct, :]
        ws_c = jnp.exp(f_l - f_c + ig_c - m_l)
        a_l = jnp.exp(f_l + m0 - m_l)
        kw = k * ws_c
        if pad_rows:
            k_p = jnp.concatenate([k, jnp.zeros((pad_rows, dk), F32)], axis=0)
            kw_p = jnp.concatenate([kw, jnp.zeros((pad_rows, dk), F32)], axis=0)
            v_p = jnp.concatenate([v, jnp.zeros((pad_rows, dv), F32)], axis=0)
        else:
            k_p, kw_p, v_p = k, kw, v
        return dict(q=q, q_b=q.astype(BF16), k_b=k_p.astype(BF16), kw=kw, kw_b=kw_p.astype(BF16),
                    v_b=v_p.astype(BF16), dmat=dmat, a_c=a_c, a_l=a_l, m_c=m_c, m_l=m_l)

    def first_matmuls(h, t):
        c_old = c_ref[h]
        t["s"] = _dot_nt(t["q_b"], t["k_b"]) * t["dmat"]
        t["qc"] = _dot(t["q_b"], c_old.astype(BF16))
        return t

    def update_memory(h, t):
        c_ref[h] = t["a_l"] * c_ref[h] + _dot_tn(t["kw_b"], t["v_b"])

    def outputs(h, t):
        n_old = n_all[h:h + 1, :]
        s, a_c = t["s"], t["a_c"]
        num = a_c * t["qc"] + _dot(s.astype(BF16), t["v_b"])
        dotn = (a_c * jnp.sum(t["q"] * n_old, axis=1, keepdims=True)
                + jnp.sum(s, axis=1, keepdims=True))
        den = jnp.maximum(jnp.abs(dotn), jnp.exp(-t["m_c"])) + EPS
        hh = num / den
        hn = hh * lax.rsqrt(jnp.mean(hh * hh, axis=1, keepdims=True) + EPS)
        hn = hn * mg_ref[:, h * dv:(h + 1) * dv]
        hm_ref[:, h * dv:(h + 1) * dv] = hn * _sigmoid(o_ref[:, h * dv:(h + 1) * dv])
        n_new = t["a_l"] * n_old + jnp.sum(t["kw"], axis=0, keepdims=True)
        return n_new, jnp.broadcast_to(t["m_l"], (1, m_ref.shape[1]))

    if pad_rows:
        ts = [gates_and_projections(h) for h in range(heads)]
        ts = [first_matmuls(h, ts[h]) for h in range(heads)]
        for h in range(heads):
            update_memory(h, ts[h])
        new = [outputs(h, ts[h]) for h in range(heads)]
    else:
        new = []
        for h in range(heads):
            t = first_matmuls(h, gates_and_projections(h))
            new.append(outputs(h, t))
            update_memory(h, t)

    for h in range(heads):
        n_ref[h:h + 1, :] = new[h][0]
        m_ref[h:h + 1, :] = new[h][1]


def _mlstm(z3, cols, gt, conv_hist, conv_w, conv_b3, wq, wk, bi3, bf3, mg3, c0, n0, m0,
           layer, state_layer, ct, cs, l_real, prev_c=None):
    b, l, _ = z3.shape
    heads, dk = wq.shape[1], wq.shape[2]
    dv = c0.shape[-1]
    qkw, vw = heads * dk, heads * dv
    nc = l // ct
    stacked = prev_c is not None
    assert cols["qk"] % qkw == 0 and cols["v"] % vw == 0 and cols["o"] % vw == 0
    body = functools.partial(_mlstm_body, ct, cs, l_real, heads, dk, dv, stacked)
    in_specs = [
        pl.BlockSpec((None, ct, qkw), lambda i, c: (i, c, cols["qk"] // qkw)),
        pl.BlockSpec((None, ct, vw), lambda i, c: (i, c, cols["v"] // vw)),
        pl.BlockSpec((None, ct, vw), lambda i, c: (i, c, cols["o"] // vw)),
        pl.BlockSpec((None, 2 * heads, cs), lambda i, c: (i, 0, c)),
        pl.BlockSpec((None, SUBLANES, qkw), lambda i, c: (i, 0, 0)),
        pl.BlockSpec((None, CONV_W, qkw), lambda i, c: (layer, 0, 0)),
        pl.BlockSpec((None, 1, qkw), lambda i, c: (layer, 0, 0)),
        pl.BlockSpec((None, heads, dk, dk), lambda i, c: (layer, 0, 0, 0)),
        pl.BlockSpec((None, heads, dk, dk), lambda i, c: (layer, 0, 0, 0)),
        pl.BlockSpec((None, heads, 1), lambda i, c: (layer, 0, 0)),
        pl.BlockSpec((None, heads, 1), lambda i, c: (layer, 0, 0)),
        pl.BlockSpec((None, 1, vw), lambda i, c: (layer, 0, 0)),
        pl.BlockSpec((None, None, heads, dk, dv), lambda i, c: (state_layer, i, 0, 0, 0)),
        pl.BlockSpec((None, None, heads, dk), lambda i, c: (state_layer, i, 0, 0)),
        pl.BlockSpec((None, None, heads, 1), lambda i, c: (state_layer, i, 0, 0)),
    ]
    out_specs = [
        pl.BlockSpec((None, ct, vw), lambda i, c: (i, c, 0)),
        pl.BlockSpec((None, heads, dk, dv), lambda i, c: (i, 0, 0, 0)),
        pl.BlockSpec((None, heads, dk), lambda i, c: (i, 0, 0)),
        pl.BlockSpec((None, heads, LANES), lambda i, c: (i, 0, 0)),
    ]
    out_shape = [
        jax.ShapeDtypeStruct((b, l, vw), F32),
        jax.ShapeDtypeStruct((b, heads, dk, dv), F32),
        jax.ShapeDtypeStruct((b, heads, dk), F32),
        jax.ShapeDtypeStruct((b, heads, LANES), F32),
    ]
    args = [z3, z3, z3, gt, conv_hist, conv_w, conv_b3, wq, wk, bi3, bf3, mg3, c0, n0, m0]
    if stacked:
        in_specs.append(pl.BlockSpec((None, heads, dk, dv), lambda i, c: (i, 0, 0, 0)))
        args.append(prev_c)
        out_specs[1] = pl.BlockSpec((2, None, heads, dk, dv), lambda i, c: (0, i, 0, 0, 0))
        out_shape[1] = jax.ShapeDtypeStruct((2, b, heads, dk, dv), F32)
    return pl.pallas_call(
        body,
        grid=(b, nc),
        in_specs=in_specs,
        out_specs=out_specs,
        out_shape=out_shape,
        scratch_shapes=[pltpu.VMEM((ct + SUBLANES, qkw), F32)],
        compiler_params=_cparams(2),
        name="mlstm",
    )(*args)


def _gmlp_body(ct, cs, groups, gd, emit_vn, zu_ref, zv_ref, ga_ref, gb_ref, hm_ref, lng_ref, lnb_ref,
               ws_ref, bst_ref, *outs):
    mg_ref = outs[0]
    gv = _gelu(zv_ref[...])
    mu = jnp.mean(gv, axis=-1, keepdims=True)
    xc = gv - mu
    var = jnp.mean(xc * xc, axis=-1, keepdims=True)
    vn = xc * lax.rsqrt(var + EPS) * lng_ref[...] + lnb_ref[...]
    if emit_vn:
        outs[1][...] = vn
    vn_b = vn.astype(BF16)
    if cs > ct:
        vn_b = jnp.concatenate([vn_b.astype(F32), jnp.zeros((cs - ct, vn.shape[1]), F32)], axis=0).astype(BF16)
    tril = (lax.broadcasted_iota(jnp.int32, (ct, cs), 1) <= lax.broadcasted_iota(jnp.int32, (ct, cs), 0))
    for g in range(groups):
        sl = slice(g * gd, (g + 1) * gd)
        w = jnp.where(tril, ws_ref[g, 0:ct, :], 0.0).astype(BF16)
        f = _dot(w, vn_b[:, sl]) + bst_ref[0:ct, g:g + 1]
        sg = _gelu(zu_ref[:, sl]) * f
        merged = _sigmoid(ga_ref[:, sl]) * hm_ref[:, sl] + _sigmoid(gb_ref[:, sl]) * sg
        mg_ref[:, sl] = merged.astype(mg_ref.dtype)


def _gmlp(z3, cols, hm, lng3, lnb3, w_s, b_st, layer, ct, cs, emit_vn, merged_dtype):
    b, l, _ = z3.shape
    gw = hm.shape[-1]
    groups = w_s.shape[1]
    gd = gw // groups
    nc = l // ct

    def zspec(name):
        assert cols[name] % gw == 0
        return pl.BlockSpec((None, ct, gw), lambda i, c: (i, c, cols[name] // gw))

    in_specs = [zspec("zu"), zspec("zv"), zspec("ga"), zspec("gb"),
                pl.BlockSpec((None, ct, gw), lambda i, c: (i, c, 0)),
                pl.BlockSpec((None, 1, gw), lambda i, c: (layer, 0, 0)),
                pl.BlockSpec((None, 1, gw), lambda i, c: (layer, 0, 0)),
                pl.BlockSpec((None, groups, cs, cs), lambda i, c: (layer, 0, 0, 0)),
                pl.BlockSpec((None, cs, groups), lambda i, c: (layer, 0, 0))]
    out_specs = [pl.BlockSpec((None, ct, gw), lambda i, c: (i, c, 0))]
    out_shape = [jax.ShapeDtypeStruct((b, l, gw), merged_dtype)]
    if emit_vn:
        out_specs.append(pl.BlockSpec((None, ct, gw), lambda i, c: (i, c, 0)))
        out_shape.append(jax.ShapeDtypeStruct((b, l, gw), F32))
    return pl.pallas_call(
        functools.partial(_gmlp_body, ct, cs, groups, gd, emit_vn),
        grid=(b, nc),
        in_specs=in_specs,
        out_specs=out_specs,
        out_shape=out_shape,
        compiler_params=_cparams(2),
        name="gmlp_merge",
    )(z3, z3, z3, z3, hm, lng3, lnb3, w_s, b_st)


def _outproj_body(a_ref, w_ref, x_ref, gate_ref, o_ref):
    acc = _dot(a_ref[...].astype(BF16), w_ref[...])
    o_ref[...] = x_ref[...] + gate_ref[...] * acc.reshape(x_ref.shape)


def _outproj(merged2, w, layer, x3, mod4, k_gate, tm, tn):
    b, l, d = x3.shape
    bb, r, nt, ridx = _row_tiling(b, l, tm)
    kdim = merged2.shape[1]
    return pl.pallas_call(
        _outproj_body,
        grid=(nt, d // tn),
        in_specs=[pl.BlockSpec((tm, kdim), lambda i, j: (i, 0)),
                  pl.BlockSpec((None, kdim, tn), lambda i, j: (layer, 0, j)),
                  pl.BlockSpec((bb, r, tn), lambda i, j: ridx(i) + (j,)),
                  pl.BlockSpec((bb, None, 1, tn), lambda i, j: (ridx(i)[0], k_gate, 0, j))],
        out_specs=pl.BlockSpec((bb, r, tn), lambda i, j: ridx(i) + (j,)),
        out_shape=jax.ShapeDtypeStruct((b, l, d), F32),
        compiler_params=_cparams(2),
        name="outproj_residual",
    )(merged2, w, x3, mod4)


def _peer_pairs():
    k = P_TOPK + 1
    return [(a, b) for a in range(k) for b in range(k) if (a + 1) * (b + 1) <= k]


def _topk_body(nk, tt, pq_ref, keys_ref, st_ref, aux_ref, topv, topi, cand, cid):
    p = pl.program_id(1) % 2
    riota = lax.broadcasted_iota(jnp.int32, (nk, tt), 0).astype(F32)
    s = _dot_nt(keys_ref[...].astype(BF16), pq_ref[...].astype(BF16))
    st_ref[...] = s
    x = s
    for a in range(P_TOPK + 1):
        m = jnp.max(x, axis=0, keepdims=True)
        idx = jnp.min(jnp.where(x == m, riota, float(nk)), axis=0, keepdims=True)
        topv[p, a:a + 1, :] = m
        topi[p, a:a + 1, :] = idx
        x = jnp.where(riota == idx, -jnp.inf, x)

    @pl.when(p == 1)
    def _():
        pairs = _peer_pairs()
        npad = cand.shape[0]
        cand[len(pairs):npad, :] = jnp.full((npad - len(pairs), tt), -jnp.inf, F32)
        cid[len(pairs):npad, :] = jnp.full((npad - len(pairs), tt), 0.0, F32)
        for r, (a, b) in enumerate(pairs):
            cand[r:r + 1, :] = topv[0, a:a + 1, :] + topv[1, b:b + 1, :]
            cid[r:r + 1, :] = topi[0, a:a + 1, :] * float(nk) + topi[1, b:b + 1, :]
        x = cand[...]
        ids = cid[...]
        big = float(nk * nk)
        mx = None
        z = None
        for a in range(P_TOPK):
            m = jnp.max(x, axis=0, keepdims=True)
            e = jnp.min(jnp.where(x == m, ids, big), axis=0, keepdims=True)
            if a == 0:
                mx = m
                z = jnp.ones_like(m)
            else:
                z = z + jnp.exp(m - mx)
            x = jnp.where(ids == e, -jnp.inf, x)
        runner_up = jnp.max(x, axis=0, keepdims=True)
        aux_ref[0:1, :] = m
        aux_ref[1:2, :] = e
        aux_ref[2:3, :] = mx + jnp.log(z)
        aux_ref[3:4, :] = jnp.where(runner_up == m, 1.0, 0.0)
        aux_ref[4:8, :] = jnp.zeros((4, tt), F32)


def _peer_topk(pq, keys4, layer, tt):
    n, qw = pq.shape
    hp, nk, half = keys4.shape[1:]
    heads = hp // 2
    npairs = -(-len(_peer_pairs()) // SUBLANES) * SUBLANES
    ntop = -(-(P_TOPK + 1) // SUBLANES) * SUBLANES
    return pl.pallas_call(
        functools.partial(_topk_body, nk, tt),
        grid=(n // tt, hp),
        in_specs=[pl.BlockSpec((tt, half), lambda i, k: (i, k)),
                  pl.BlockSpec((None, None, nk, half), lambda i, k: (layer, k, 0, 0))],
        out_specs=[pl.BlockSpec((None, nk, tt), lambda i, k: (k, 0, i)),
                   pl.BlockSpec((None, SUBLANES, tt), lambda i, k: (k // 2, 0, i))],
        out_shape=[jax.ShapeDtypeStruct((hp, nk, n), F32),
                   jax.ShapeDtypeStruct((heads, SUBLANES, n), F32)],
        scratch_shapes=[pltpu.VMEM((2, ntop, tt), F32), pltpu.VMEM((2, ntop, tt), F32),
                        pltpu.VMEM((npairs, tt), F32), pltpu.VMEM((npairs, tt), F32)],
        compiler_params=_cparams(2),
        name="peer_topk",
    )(pq, keys4)


def _peer_body(heads, nk, tt, ec, tie_ref, h2_ref, st_ref, aux_ref, u_ref, v_ref, o_ref, at_scr, w_scr, p_scr):
    j = pl.program_id(1)
    rt = 4 * SUBLANES
    rows = lax.broadcasted_iota(jnp.int32, (rt, LANES), 0)
    per = ec // nk
    grp = SUBLANES // per
    base = pl.multiple_of((j // grp) * SUBLANES, SUBLANES)
    sub = j % grp

    def weights(exact_ties):
        for ii in range(per):
            i_glob = j * per + ii
            for lb in range(tt // LANES):
                ln = slice(lb * LANES, (lb + 1) * LANES)
                for r in range(nk // rt):
                    wacc = jnp.zeros((rt, LANES), F32)
                    for h in range(heads):
                        tile = st_ref[2 * h, pl.ds(base, SUBLANES), ln]
                        s0b = tile[ii:ii + 1]
                        for k in range(1, grp):
                            s0b = jnp.where(sub == k, tile[k * per + ii:k * per + ii + 1], s0b)
                        cnd = st_ref[2 * h + 1, r * rt:(r + 1) * rt, ln] + s0b
                        thr = aux_ref[h, 0:1, ln]
                        c0 = aux_ref[h, 2:3, ln]
                        if exact_ties:
                            eio = (i_glob * nk + r * rt + rows).astype(F32)
                            sel = (cnd > thr) | ((cnd == thr) & (eio <= aux_ref[h, 1:2, ln]))
                        else:
                            sel = cnd >= thr
                        wacc = wacc + jnp.where(sel, jnp.exp(cnd - c0), 0.0)
                    w_scr[ii * nk + r * rt:ii * nk + (r + 1) * rt, ln] = wacc

    tie = tie_ref[pl.program_id(0)]

    @pl.when(tie == 0)
    def _():
        weights(False)

    @pl.when(tie != 0)
    def _():
        weights(True)

    at_scr[...] = _dot_nt(u_ref[...], h2_ref[...])
    for ii in range(per):
        for lb in range(tt // LANES):
            ln = slice(lb * LANES, (lb + 1) * LANES)
            pt = w_scr[ii * nk:(ii + 1) * nk, ln] * _gelu(at_scr[ii * nk:(ii + 1) * nk, ln])
            p_scr[ln, ii * nk:(ii + 1) * nk] = pt.T.astype(BF16)

    @pl.when(j == 0)
    def _():
        o_ref[...] = jnp.zeros(o_ref.shape, F32)

    d = o_ref.shape[1]
    cw = min(d, 1024)
    for cb in range(d // cw):
        o_ref[:, cb * cw:(cb + 1) * cw] += _dot(p_scr[...], v_ref[:, cb * cw:(cb + 1) * cw])


def _peer_dense(h2, st, aux, tie, u_tab, v_tab, layer, tt, ec):
    n, d = h2.shape
    e = u_tab.shape[1]
    hp, nk, _ = st.shape
    heads = hp // 2
    assert nk == LANES and e == nk * nk and ec % nk == 0 and tt % LANES == 0
    assert SUBLANES % (ec // nk) == 0
    return pl.pallas_call(
        functools.partial(_peer_body, heads, nk, tt, ec),
        grid_spec=pltpu.PrefetchScalarGridSpec(
            num_scalar_prefetch=1,
            grid=(n // tt, e // ec),
            in_specs=[pl.BlockSpec((tt, d), lambda i, j, f: (i, 0)),
                      pl.BlockSpec((hp, nk, tt), lambda i, j, f: (0, 0, i)),
                      pl.BlockSpec((heads, SUBLANES, tt), lambda i, j, f: (0, 0, i)),
                      pl.BlockSpec((None, ec, d), lambda i, j, f: (layer, j, 0)),
                      pl.BlockSpec((None, ec, d), lambda i, j, f: (layer, j, 0))],
            out_specs=pl.BlockSpec((tt, d), lambda i, j, f: (i, 0)),
            scratch_shapes=[pltpu.VMEM((ec, tt), F32), pltpu.VMEM((ec, tt), F32), pltpu.VMEM((tt, ec), BF16)]),
        out_shape=jax.ShapeDtypeStruct((n, d), F32),
        compiler_params=_cparams(2),
        name="peer_dense",
    )(tie, h2, st, aux, u_tab, v_tab)


def _resid_body(final, x_ref, p_ref, gate_ref, *rest):
    x = x_ref[...] + gate_ref[...] * p_ref[...]
    if final:
        gf_ref, o_ref = rest
        o_ref[...] = x * lax.rsqrt(jnp.mean(x * x, axis=-1, keepdims=True) + EPS) * gf_ref[...]
    else:
        rest[0][...] = x


def _residual(x3, p3, mod4, k_gate, tm, g_final2=None):
    b, l, d = x3.shape
    bb, r, nt, ridx = _row_tiling(b, l, tm)
    final = g_final2 is not None
    in_specs = [pl.BlockSpec((bb, r, d), lambda i: ridx(i) + (0,)),
                pl.BlockSpec((bb, r, d), lambda i: ridx(i) + (0,)),
                pl.BlockSpec((bb, None, 1, d), lambda i: (ridx(i)[0], k_gate, 0, 0))]
    args = [x3, p3, mod4]
    if final:
        in_specs.append(pl.BlockSpec((1, d), lambda i: (0, 0)))
        args.append(g_final2)
    return pl.pallas_call(
        functools.partial(_resid_body, final),
        grid=(nt,),
        in_specs=in_specs,
        out_specs=pl.BlockSpec((bb, r, d), lambda i: ridx(i) + (0,)),
        out_shape=jax.ShapeDtypeStruct((b, l, d), F32),
        compiler_params=_cparams(1),
        name="residual",
    )(*args)


def _pick(n, pref):
    t = min(n, pref)
    while n % t:
        t //= 2
    return t


def _trunk_layer(x3, mod4, l_real, conv_hist, c0, n0, m0, state_layer, lw, layer, g_final2, want_vn,
                 prev_c=None):
    b, l, d = x3.shape
    n = b * l
    heads, dk = lw["wq"].shape[1], lw["wq"].shape[2]
    qkw = heads * dk
    cols = lw["cols"]
    chunk = LANES
    ct = min(chunk, l)
    tm = _pick(n, 512)

    z, gates = _modnorm_matmul(x3, lw["g_mix"], mod4, 1, 0, lw["w_in"], layer, tm,
                               _pick(lw["w_in"].shape[2], 1024), w_small=lw["w_gate"])
    z3 = z.reshape(b, l, -1)
    gt = jnp.swapaxes(gates[:, :2 * heads].reshape(b, l, 2 * heads), 1, 2)
    if l % chunk:
        gt = jnp.pad(gt, ((0, 0), (0, 0), (0, chunk - l % chunk)))
    conv_new = z3[:, l_real - (CONV_W - 1):l_real, cols["qk"]:cols["qk"] + qkw]
    hm, c1, n1, m1 = _mlstm(z3, cols, gt, conv_hist, lw["conv_w"], lw["conv_b"], lw["wq"], lw["wk"],
                            lw["b_i"], lw["b_f"], lw["mnorm_g"], c0, n0, m0,
                            layer, state_layer, ct, chunk, l_real, prev_c)
    outs = _gmlp(z3, cols, hm, lw["ln_g"], lw["ln_b"], lw["w_s"], lw["b_st"], layer, ct, chunk,
                 want_vn, BF16 if ct % 16 == 0 else F32)
    merged = outs[0]
    vn = outs[1] if want_vn else None
    x3 = _outproj(merged.reshape(n, -1), lw["w_out"], layer, x3, mod4, 2, tm, _pick(d, 1024))

    pq, h2 = _modnorm_matmul(x3, lw["g_ffn"], mod4, 4, 3, lw["w_pq"], layer, tm,
                             _pick(lw["w_pq"].shape[2], 512), emit_h=True)
    if l_real < l:
        pq = pq.reshape(b, l, -1)[:, :l_real].reshape(b * l_real, -1)
        h2 = h2.reshape(b, l, -1)[:, :l_real].reshape(b * l_real, -1)
    nr = b * l_real
    tt = _pick(nr, 512)
    st, aux = _peer_topk(pq, lw["keys"], layer, tt)
    tie = (jnp.max(aux[:, 3, :].reshape(-1, nr // tt, tt), axis=(0, 2)) > 0).astype(jnp.int32)
    peer = _peer_dense(h2, st, aux, tie, lw["u_tab"], lw["v_tab"], layer, tt,
                       _pick(lw["u_tab"].shape[1], 512))
    peer = peer.reshape(b, l_real, d)
    if l_real < l:
        peer = jnp.pad(peer, ((0, 0), (0, l - l_real), (0, 0)))
    x3 = _residual(x3, peer, mod4, 5, _pick(n, 256), g_final2)
    return x3, conv_new, c1, n1, m1[:, :, 0], vn


def kernel(x_prompt, x_sample, state_C, state_n, state_m, state_conv, c_prompt, c_sample, w_ada, b_ada, g_mix, w_in, conv_w, conv_b, wq_m, wk_m, b_i, b_f, mnorm_g, ln_g, ln_b, w_s, b_s, w_out, g_ffn, w_pq, sub_keys, u_tab, v_tab, g_final):
    depth = w_ada.shape[0]
    bp, lp, d = x_prompt.shape
    bs, ls, _ = x_sample.shape
    heads, dk = wq_m.shape[1], wq_m.shape[2]
    dv = state_C.shape[-1]
    qkw, vw = heads * dk, heads * dv
    gw = ln_g.shape[1]

    ls_pad = -(-ls // SUBLANES) * SUBLANES
    xs = jnp.pad(x_sample, ((0, 0), (0, ls_pad - ls), (0, 0)))
    xp = x_prompt

    n_c = bp + bs
    n_c_pad = -(-n_c // SUBLANES) * SUBLANES
    c_all = jnp.pad(jnp.concatenate([c_prompt, c_sample], axis=0), ((0, n_c_pad - n_c), (0, 0)))

    o_v, o_o, o_g = qkw, qkw + vw, qkw + 2 * vw
    o_zu = o_g + 2 * heads
    cols = {"v": 0, "o": vw, "zu": 2 * vw, "zv": 2 * vw + gw, "ga": 2 * vw + 2 * gw,
            "gb": 2 * vw + 2 * gw + d, "qk": 2 * vw + 2 * gw + 2 * d}

    hist_pad = SUBLANES - (CONV_W - 1)
    zero_hist = jnp.zeros((bp, SUBLANES, qkw), F32)
    zero_c = jnp.zeros((1, bp, heads, dk, dv), F32)
    zero_n = jnp.zeros((1, bp, heads, dk), F32)
    zero_m = jnp.zeros((1, bp, heads, 1), F32)
    state_m4 = state_m[..., None]
    g_final2 = g_final.reshape(1, d)
    b_ada3 = b_ada[:, None, :]

    p_out = [[], [], [], []]
    s_out = [[], [], [], [], []]
    lw = {
        "cols": cols,
        "g_mix": g_mix[:, None, :], "g_ffn": g_ffn[:, None, :],
        "w_in": jnp.concatenate([w_in[:, :, o_v:o_g].astype(BF16), w_in[:, :, o_zu:].astype(BF16),
                                 w_in[:, :, :qkw].astype(BF16)], axis=2),
        "w_gate": jnp.pad(w_in[:, :, o_g:o_zu], ((0, 0), (0, 0), (0, LANES - 2 * heads))).astype(BF16),
        "conv_w": conv_w, "conv_b": conv_b[:, None, :],
        "wq": wq_m.astype(BF16), "wk": wk_m.astype(BF16),
        "b_i": b_i[:, :, None], "b_f": b_f[:, :, None],
        "mnorm_g": mnorm_g[:, None, :], "ln_g": ln_g[:, None, :], "ln_b": ln_b[:, None, :],
        "w_s": w_s, "b_st": jnp.swapaxes(b_s, 1, 2),
        "w_out": w_out.astype(BF16), "w_pq": w_pq.astype(BF16),
        "keys": sub_keys.reshape(depth, -1, sub_keys.shape[3], sub_keys.shape[4]),
        "u_tab": u_tab.astype(BF16), "v_tab": v_tab.astype(BF16),
    }
    for l in range(depth):
        mod = _adaln(c_all, w_ada, b_ada3, l, _pick(w_ada.shape[2], 512))
        mod_p = mod[:bp].reshape(bp, N_MOD, 1, d)
        mod_s = mod[bp:n_c].reshape(bs, N_MOD, 1, d)
        last = l == depth - 1

        xp, cv, c1, n1, m1, _ = _trunk_layer(xp, mod_p, lp, zero_hist, zero_c, zero_n, zero_m, 0,
                                             lw, l, g_final2 if last else None, False)
        for acc, val in zip(p_out, (c1, n1, m1, cv)):
            acc.append(val)

        hist_s = jnp.pad(state_conv[l], ((0, 0), (hist_pad, 0), (0, 0)))
        stack_c = last and depth == 2
        xs, cv, c1, n1, m1, vn = _trunk_layer(xs, mod_s, ls, hist_s, state_C, state_n, state_m4, l,
                                              lw, l, g_final2 if last else None, True,
                                              s_out[0][0] if stack_c else None)
        if stack_c:
            sample_c = c1
        for acc, val in zip(s_out, (c1, n1, m1, cv, vn[:, :ls])):
            acc.append(val)

    s_stacked = [jnp.stack(a) for a in s_out[1:]]
    s_stacked.insert(0, sample_c if depth == 2 else jnp.stack(s_out[0]))
    return (xp, xs[:, :ls]) + tuple(jnp.stack(a) for a in p_out) + tuple(s_stacked)
```

```python
import functools

import jax
import jax.numpy as jnp
from jax import lax
from jax.experimental import pallas as pl
from jax.experimental.pallas import tpu as pltpu

F32 = jnp.float32
BF16 = jnp.bfloat16
EPS = 1e-6
NEG_BIG = -1e30
LANES = 128
SUBLANES = 8
VMEM_LIMIT = 56 * 1024 * 1024
CONV_W = 4
P_TOPK = 16
N_MOD = 6
SQRT_HALF = 0.7071067811865476


def _cparams(n_axes):
    return pltpu.CompilerParams(dimension_semantics=("arbitrary",) * n_axes,
                                vmem_limit_bytes=VMEM_LIMIT)


def _gelu(x):
    return 0.5 * x * (1.0 + lax.erf(x * SQRT_HALF))


def _sigmoid(x):
    return 1.0 / (1.0 + jnp.exp(-x))


def _log_sigmoid(x):
    return jnp.minimum(x, 0.0) - jnp.log(1.0 + jnp.exp(-jnp.abs(x)))


def _dot(a, b):
    return jnp.dot(a, b, preferred_element_type=F32)


def _dot_nt(a, b):
    return lax.dot_general(a, b, (((1,), (1,)), ((), ())), preferred_element_type=F32)


def _dot_tn(a, b):
    return lax.dot_general(a, b, (((0,), (0,)), ((), ())), preferred_element_type=F32)


def _split3(a):
    a1 = a.astype(BF16)
    r1 = a - a1.astype(F32)
    a2 = r1.astype(BF16)
    a3 = (r1 - a2.astype(F32)).astype(BF16)
    return a1, a2, a3


def _adaln_body(c_ref, w_ref, b_ref, o_ref):
    c = c_ref[...]
    a = c * _sigmoid(c)
    a1 = a.astype(BF16)
    a2 = (a - a1.astype(F32)).astype(BF16)
    w1 = w_ref[...].astype(BF16)
    o_ref[...] = _dot(a1, w1) + _dot(a2, w1) + b_ref[...]


def _adaln(c_all, w_ada, b_ada3, layer, tn):
    m, d = c_all.shape
    n = w_ada.shape[2]
    return pl.pallas_call(
        _adaln_body,
        grid=(n // tn,),
        in_specs=[pl.BlockSpec((m, d), lambda j: (0, 0)),
                  pl.BlockSpec((None, d, tn), lambda j: (layer, 0, j)),
                  pl.BlockSpec((None, 1, tn), lambda j: (layer, 0, j))],
        out_specs=pl.BlockSpec((m, tn), lambda j: (0, j)),
        out_shape=jax.ShapeDtypeStruct((m, n), F32),
        compiler_params=_cparams(1),
        name="adaln",
    )(c_all, w_ada, b_ada3)


def _row_tiling(b, l, tm):
    if l >= tm:
        assert l % tm == 0
        per = l // tm
        return 1, tm, b * per, (lambda i: (i // per, i % per))
    assert tm % l == 0 and b % (tm // l) == 0
    bb = tm // l
    return bb, l, b // bb, (lambda i: (i, 0))


def _modnorm_body(has_small, emit_h, tm, x_ref, g_ref, sc_ref, sh_ref, w_ref, *rest):
    rest = list(rest)
    ws_ref = rest.pop(0) if has_small else None
    o_ref = rest.pop(0)
    os_ref = rest.pop(0) if has_small else None
    ho_ref = rest.pop(0) if emit_h else None
    h_scr = rest.pop(0)

    @pl.when(pl.program_id(1) == 0)
    def _():
        bb, r, d = x_ref.shape
        nsub = 4
        while bb > 1 and bb % nsub:
            nsub //= 2
        rows = tm // nsub

        def sub(k, carry):
            if bb == 1:
                sl = (slice(None), pl.ds(pl.multiple_of(k * rows, rows), rows), slice(None))
                sc, sh = sc_ref[...], sh_ref[...]
            else:
                bsub = pl.ds(pl.multiple_of(k * (bb // nsub), bb // nsub), bb // nsub)
                sl = (bsub, slice(None), slice(None))
                sc, sh = sc_ref[bsub], sh_ref[bsub]
            x = x_ref[sl]
            var = jnp.mean(x * x, axis=-1, keepdims=True)
            y = x * lax.rsqrt(var + EPS) * g_ref[...]
            h = y * (1.0 + sc) + sh
            h_scr[pl.ds(pl.multiple_of(k * rows, rows), rows), :] = h.reshape(rows, d).astype(BF16)
            return carry

        lax.fori_loop(0, nsub, sub, 0)
        if has_small:
            os_ref[...] = _dot(h_scr[...], ws_ref[...])
        if emit_h:
            ho_ref[...] = h_scr[...]

    o_ref[...] = _dot(h_scr[...], w_ref[...])


def _modnorm_matmul(x3, g3, mod4, k_scale, k_shift, w, layer, tm, tn, w_small=None, emit_h=False):
    b, l, d = x3.shape
    nout = w.shape[2]
    bb, r, nt, ridx = _row_tiling(b, l, tm)
    has_small = w_small is not None

    def xmap(i, j):
        bi, ri = ridx(i)
        return (bi, ri, 0)

    def modmap(k):
        def f(i, j):
            return (ridx(i)[0], k, 0, 0)
        return f

    in_specs = [pl.BlockSpec((bb, r, d), xmap),
                pl.BlockSpec((None, 1, d), lambda i, j: (layer, 0, 0)),
                pl.BlockSpec((bb, None, 1, d), modmap(k_scale)),
                pl.BlockSpec((bb, None, 1, d), modmap(k_shift)),
                pl.BlockSpec((None, d, tn), lambda i, j: (layer, 0, j))]
    args = [x3, g3, mod4, mod4, w]
    out_specs = [pl.BlockSpec((tm, tn), lambda i, j: (i, j))]
    out_shape = [jax.ShapeDtypeStruct((b * l, nout), F32)]
    if has_small:
        ns = w_small.shape[2]
        in_specs.append(pl.BlockSpec((None, d, ns), lambda i, j: (layer, 0, 0)))
        args.append(w_small)
        out_specs.append(pl.BlockSpec((tm, ns), lambda i, j: (i, 0)))
        out_shape.append(jax.ShapeDtypeStruct((b * l, ns), F32))
    if emit_h:
        out_specs.append(pl.BlockSpec((tm, d), lambda i, j: (i, 0)))
        out_shape.append(jax.ShapeDtypeStruct((b * l, d), BF16))
    return pl.pallas_call(
        functools.partial(_modnorm_body, has_small, emit_h, tm),
        grid=(nt, nout // tn),
        in_specs=in_specs,
        out_specs=out_specs,
        out_shape=out_shape,
        scratch_shapes=[pltpu.VMEM((tm, d), BF16)],
        compiler_params=_cparams(2),
        name="modnorm_matmul",
    )(*args)


def _mlstm_body(ct, cs, l_real, heads, dk, dv, stacked,
                qk_ref, v_ref, o_ref, gt_ref, cb_ref, cw_ref, cbias_ref, wq_ref, wk_ref,
                bi_ref, bf_ref, mg_ref, c0_ref, n0_ref, m0_ref, *rest):
    if stacked:
        prev_ref, hm_ref, call_ref, n_ref, m_ref, xbuf = rest
        c_ref = call_ref.at[1]
    else:
        hm_ref, c_ref, n_ref, m_ref, xbuf = rest
    hist = SUBLANES

    @pl.when(pl.program_id(1) == 0)
    def _():
        if stacked:
            call_ref[0] = prev_ref[...]
        c_ref[...] = c0_ref[...]
        n_ref[...] = n0_ref[...]
        m_ref[...] = jnp.broadcast_to(m0_ref[...], m_ref.shape)
        xbuf[0:hist, :] = cb_ref[...]

    xbuf[hist:hist + ct, :] = qk_ref[...]
    acc = jnp.broadcast_to(cbias_ref[...], (ct, heads * dk))
    for j in range(CONV_W):
        acc = acc + xbuf[pl.ds(hist - (CONV_W - 1) + j, ct), :] * cw_ref[j:j + 1, :]
    xc = (acc * _sigmoid(acc)).astype(BF16)
    xbuf[0:hist, :] = xbuf[ct:ct + hist, :]

    g = gt_ref[...]
    ig_all = g[0:heads] + bi_ref[...]
    lf_all = _log_sigmoid(g[heads:2 * heads] + bf_ref[...])
    if l_real < cs:
        valid = lax.broadcasted_iota(jnp.int32, (heads, cs), 1) < l_real
        ig_all = jnp.where(valid, ig_all, NEG_BIG)
        lf_all = jnp.where(valid, lf_all, 0.0)
    upper = (lax.broadcasted_iota(jnp.int32, (cs, cs), 0)
             <= lax.broadcasted_iota(jnp.int32, (cs, cs), 1)).astype(BF16)
    l1, l2, l3 = _split3(lf_all)
    f_all = _dot(l1, upper) + _dot(l2, upper) + _dot(l3, upper)

    tt = lax.broadcasted_iota(jnp.int32, (ct, cs), 0)
    ss = lax.broadcasted_iota(jnp.int32, (ct, cs), 1)
    diag = tt == ss
    tril = ss <= tt
    pad_rows = cs - ct

    m_all = m_ref[...]
    n_all = n_ref[...]

    def gates_and_projections(h):
        ig_r = ig_all[h:h + 1, :]
        f_r = f_all[h:h + 1, :]
        m0 = m_all[h:h + 1, 0:1]
        f_c = jnp.sum(jnp.where(diag, f_r, 0.0), axis=1, keepdims=True)
        ig_c = jnp.sum(jnp.where(diag, ig_r, 0.0), axis=1, keepdims=True)
        cm_c = jnp.max(jnp.where(tril, ig_r - f_r, -jnp.inf), axis=1, keepdims=True)
        m_c = f_c + jnp.maximum(m0, cm_c)
        dmat = jnp.exp(jnp.where(tril, f_c - f_r + ig_r - m_c, -jnp.inf))
        a_c = jnp.exp(f_c + m0 - m_c)
        xh = xc[:, h * dk:(h + 1) * dk]
        q = _dot(xh, wq_ref[h])
        k = _dot(xh, wk_ref[h]) * (dk ** -0.5)
        v = v_ref[:, h * dv:(h + 1) * dv]
        m_l = m_c[ct - 1:ct, :]
        f_l = f_c[ct - 1:ct, :]
        ws_c = jnp.exp(f_l - f_c + ig_c - m_l)
        a_l = jnp.exp(f_l + m0 - m_l)
        kw = k * ws_c
        if pad_rows:
            k_p = jnp.concatenate([k, jnp.zeros((pad_rows, dk), F32)], axis=0)
            kw_p = jnp.concatenate([kw, jnp.zeros((pad_rows, dk), F32)], axis=0)
            v_p = jnp.concatenate([v, jnp.zeros((pad_rows, dv), F32)], axis=0)
        else:
            k_p, kw_p, v_p = k, kw, v
        return dict(q=q, q_b=q.astype(BF16), k_b=k_p.astype(BF16), kw=kw, kw_b=kw_p.astype(BF16),
                    v_b=v_p.astype(BF16), dmat=dmat, a_c=a_c, a_l=a_l, m_c=m_c, m_l=m_l)

    def first_matmuls(h, t):
        c_old = c_ref[h]
        t["s"] = _dot_nt(t["q_b"], t["k_b"]) * t["dmat"]
        t["qc"] = _dot(t["q_b"], c_old.astype(BF16))
        return t

    def update_memory(h, t):
        c_ref[h] = t["a_l"] * c_ref[h] + _dot_tn(t["kw_b"], t["v_b"])

    def outputs(h, t):
        n_old = n_all[h:h + 1, :]
        s, a_c = t["s"], t["a_c"]
        num = a_c * t["qc"] + _dot(s.astype(BF16), t["v_b"])
        dotn = (a_c * jnp.sum(t["q"] * n_old, axis=1, keepdims=True)
                + jnp.sum(s, axis=1, keepdims=True))
        den = jnp.maximum(jnp.abs(dotn), jnp.exp(-t["m_c"])) + EPS
        hh = num / den
        hn = hh * lax.rsqrt(jnp.mean(hh * hh, axis=1, keepdims=True) + EPS)
        hn = hn * mg_ref[:, h * dv:(h + 1) * dv]
        hm_ref[:, h * dv:(h + 1) * dv] = hn * _sigmoid(o_ref[:, h * dv:(h + 1) * dv])
        n_new = t["a_l"] * n_old + jnp.sum(t["kw"], axis=0, keepdims=True)
        return n_new, jnp.broadcast_to(t["m_l"], (1, m_ref.shape[1]))

    if pad_rows:
        ts = [gates_and_projections(h) for h in range(heads)]
        ts = [first_matmuls(h, ts[h]) for h in range(heads)]
        for h in range(heads):
            update_memory(h, ts[h])
        new = [outputs(h, ts[h]) for h in range(heads)]
    else:
        new = []
        for h in range(heads):
            t = first_matmuls(h, gates_and_projections(h))
            new.append(outputs(h, t))
            update_memory(h, t)

    for h in range(heads):
        n_ref[h:h + 1, :] = new[h][0]
        m_ref[h:h + 1, :] = new[h][1]


def _mlstm(z3, cols, gt, conv_hist, conv_w, conv_b3, wq, wk, bi3, bf3, mg3, c0, n0, m0,
           layer, state_layer, ct, cs, l_real, prev_c=None):
    b, l, _ = z3.shape
    heads, dk = wq.shape[1], wq.shape[2]
    dv = c0.shape[-1]
    qkw, vw = heads * dk, heads * dv
    nc = l // ct
    stacked = prev_c is not None
    assert cols["qk"] % qkw == 0 and cols["v"] % vw == 0 and cols["o"] % vw == 0
    body = functools.partial(_mlstm_body, ct, cs, l_real, heads, dk, dv, stacked)
    in_specs = [
        pl.BlockSpec((None, ct, qkw), lambda i, c: (i, c, cols["qk"] // qkw)),
        pl.BlockSpec((None, ct, vw), lambda i, c: (i, c, cols["v"] // vw)),
        pl.BlockSpec((None, ct, vw), lambda i, c: (i, c, cols["o"] // vw)),
        pl.BlockSpec((None, 2 * heads, cs), lambda i, c: (i, 0, c)),
        pl.BlockSpec((None, SUBLANES, qkw), lambda i, c: (i, 0, 0)),
        pl.BlockSpec((None, CONV_W, qkw), lambda i, c: (layer, 0, 0)),
        pl.BlockSpec((None, 1, qkw), lambda i, c: (layer, 0, 0)),
        pl.BlockSpec((None, heads, dk, dk), lambda i, c: (layer, 0, 0, 0)),
        pl.BlockSpec((None, heads, dk, dk), lambda i, c: (layer, 0, 0, 0)),
        pl.BlockSpec((None, heads, 1), lambda i, c: (layer, 0, 0)),
        pl.BlockSpec((None, heads, 1), lambda i, c: (layer, 0, 0)),
        pl.BlockSpec((None, 1, vw), lambda i, c: (layer, 0, 0)),
        pl.BlockSpec((None, None, heads, dk, dv), lambda i, c: (state_layer, i, 0, 0, 0)),
        pl.BlockSpec((None, None, heads, dk), lambda i, c: (state_layer, i, 0, 0)),
        pl.BlockSpec((None, None, heads, 1), lambda i, c: (state_layer, i, 0, 0)),
    ]
    out_specs = [
        pl.BlockSpec((None, ct, vw), lambda i, c: (i, c, 0)),
        pl.BlockSpec((None, heads, dk, dv), lambda i, c: (i, 0, 0, 0)),
        pl.BlockSpec((None, heads, dk), lambda i, c: (i, 0, 0)),
        pl.BlockSpec((None, heads, LANES), lambda i, c: (i, 0, 0)),
    ]
    out_shape = [
        jax.ShapeDtypeStruct((b, l, vw), F32),
        jax.ShapeDtypeStruct((b, heads, dk, dv), F32),
        jax.ShapeDtypeStruct((b, heads, dk), F32),
        jax.ShapeDtypeStruct((b, heads, LANES), F32),
    ]
    args = [z3, z3, z3, gt, conv_hist, conv_w, conv_b3, wq, wk, bi3, bf3, mg3, c0, n0, m0]
    if stacked:
        in_specs.append(pl.BlockSpec((None, heads, dk, dv), lambda i, c: (i, 0, 0, 0)))
        args.append(prev_c)
        out_specs[1] = pl.BlockSpec((2, None, heads, dk, dv), lambda i, c: (0, i, 0, 0, 0))
        out_shape[1] = jax.ShapeDtypeStruct((2, b, heads, dk, dv), F32)
    return pl.pallas_call(
        body,
        grid=(b, nc),
        in_specs=in_specs,
        out_specs=out_specs,
        out_shape=out_shape,
        scratch_shapes=[pltpu.VMEM((ct + SUBLANES, qkw), F32)],
        compiler_params=_cparams(2),
        name="mlstm",
    )(*args)


def _gmlp_body(ct, cs, groups, gd, emit_vn, zu_ref, zv_ref, ga_ref, gb_ref, hm_ref, lng_ref, lnb_ref,
               ws_ref, bst_ref, *outs):
    mg_ref = outs[0]
    gv = _gelu(zv_ref[...])
    mu = jnp.mean(gv, axis=-1, keepdims=True)
    xc = gv - mu
    var = jnp.mean(xc * xc, axis=-1, keepdims=True)
    vn = xc * lax.rsqrt(var + EPS) * lng_ref[...] + lnb_ref[...]
    if emit_vn:
        outs[1][...] = vn
    vn_b = vn.astype(BF16)
    if cs > ct:
        vn_b = jnp.concatenate([vn_b.astype(F32), jnp.zeros((cs - ct, vn.shape[1]), F32)], axis=0).astype(BF16)
    tril = (lax.broadcasted_iota(jnp.int32, (ct, cs), 1) <= lax.broadcasted_iota(jnp.int32, (ct, cs), 0))
    for g in range(groups):
        sl = slice(g * gd, (g + 1) * gd)
        w = jnp.where(tril, ws_ref[g, 0:ct, :], 0.0).astype(BF16)
        f = _dot(w, vn_b[:, sl]) + bst_ref[0:ct, g:g + 1]
        sg = _gelu(zu_ref[:, sl]) * f
        merged = _sigmoid(ga_ref[:, sl]) * hm_ref[:, sl] + _sigmoid(gb_ref[:, sl]) * sg
        mg_ref[:, sl] = merged.astype(mg_ref.dtype)


def _gmlp(z3, cols, hm, lng3, lnb3, w_s, b_st, layer, ct, cs, emit_vn, merged_dtype):
    b, l, _ = z3.shape
    gw = hm.shape[-1]
    groups = w_s.shape[1]
    gd = gw // groups
    nc = l // ct

    def zspec(name):
        assert cols[name] % gw == 0
        return pl.BlockSpec((None, ct, gw), lambda i, c: (i, c, cols[name] // gw))

    in_specs = [zspec("zu"), zspec("zv"), zspec("ga"), zspec("gb"),
                pl.BlockSpec((None, ct, gw), lambda i, c: (i, c, 0)),
                pl.BlockSpec((None, 1, gw), lambda i, c: (layer, 0, 0)),
                pl.BlockSpec((None, 1, gw), lambda i, c: (layer, 0, 0)),
                pl.BlockSpec((None, groups, cs, cs), lambda i, c: (layer, 0, 0, 0)),
                pl.BlockSpec((None, cs, groups), lambda i, c: (layer, 0, 0))]
    out_specs = [pl.BlockSpec((None, ct, gw), lambda i, c: (i, c, 0))]
    out_shape = [jax.ShapeDtypeStruct((b, l, gw), merged_dtype)]
    if emit_vn:
        out_specs.append(pl.BlockSpec((None, ct, gw), lambda i, c: (i, c, 0)))
        out_shape.append(jax.ShapeDtypeStruct((b, l, gw), F32))
    return pl.pallas_call(
        functools.partial(_gmlp_body, ct, cs, groups, gd, emit_vn),
        grid=(b, nc),
        in_specs=in_specs,
        out_specs=out_specs,
        out_shape=out_shape,
        compiler_params=_cparams(2),
        name="gmlp_merge",
    )(z3, z3, z3, z3, hm, lng3, lnb3, w_s, b_st)


def _outproj_body(a_ref, w_ref, x_ref, gate_ref, o_ref):
    acc = _dot(a_ref[...].astype(BF16), w_ref[...])
    o_ref[...] = x_ref[...] + gate_ref[...] * acc.reshape(x_ref.shape)


def _outproj(merged2, w, layer, x3, mod4, k_gate, tm, tn):
    b, l, d = x3.shape
    bb, r, nt, ridx = _row_tiling(b, l, tm)
    kdim = merged2.shape[1]
    return pl.pallas_call(
        _outproj_body,
        grid=(nt, d // tn),
        in_specs=[pl.BlockSpec((tm, kdim), lambda i, j: (i, 0)),
                  pl.BlockSpec((None, kdim, tn), lambda i, j: (layer, 0, j)),
                  pl.BlockSpec((bb, r, tn), lambda i, j: ridx(i) + (j,)),
                  pl.BlockSpec((bb, None, 1, tn), lambda i, j: (ridx(i)[0], k_gate, 0, j))],
        out_specs=pl.BlockSpec((bb, r, tn), lambda i, j: ridx(i) + (j,)),
        out_shape=jax.ShapeDtypeStruct((b, l, d), F32),
        compiler_params=_cparams(2),
        name="outproj_residual",
    )(merged2, w, x3, mod4)


def _peer_pairs():
    k = P_TOPK + 1
    return [(a, b) for a in range(k) for b in range(k) if (a + 1) * (b + 1) <= k]


def _topk_body(nk, tt, pq_ref, keys_ref, st_ref, aux_ref, topv, topi, cand, cid):
    p = pl.program_id(1) % 2
    riota = lax.broadcasted_iota(jnp.int32, (nk, tt), 0).astype(F32)
    s = _dot_nt(keys_ref[...].astype(BF16), pq_ref[...].astype(BF16))
    st_ref[...] = s
    x = s
    for a in range(P_TOPK + 1):
        m = jnp.max(x, axis=0, keepdims=True)
        idx = jnp.min(jnp.where(x == m, riota, float(nk)), axis=0, keepdims=True)
        topv[p, a:a + 1, :] = m
        topi[p, a:a + 1, :] = idx
        x = jnp.where(riota == idx, -jnp.inf, x)

    @pl.when(p == 1)
    def _():
        pairs = _peer_pairs()
        npad = cand.shape[0]
        cand[len(pairs):npad, :] = jnp.full((npad - len(pairs), tt), -jnp.inf, F32)
        cid[len(pairs):npad, :] = jnp.full((npad - len(pairs), tt), 0.0, F32)
        for r, (a, b) in enumerate(pairs):
            cand[r:r + 1, :] = topv[0, a:a + 1, :] + topv[1, b:b + 1, :]
            cid[r:r + 1, :] = topi[0, a:a + 1, :] * float(nk) + topi[1, b:b + 1, :]
        x = cand[...]
        ids = cid[...]
        big = float(nk * nk)
        mx = None
        z = None
        for a in range(P_TOPK):
            m = jnp.max(x, axis=0, keepdims=True)
            e = jnp.min(jnp.where(x == m, ids, big), axis=0, keepdims=True)
            if a == 0:
                mx = m
                z = jnp.ones_like(m)
            else:
                z = z + jnp.exp(m - mx)
            x = jnp.where(ids == e, -jnp.inf, x)
        runner_up = jnp.max(x, axis=0, keepdims=True)
        aux_ref[0:1, :] = m
        aux_ref[1:2, :] = e
        aux_ref[2:3, :] = mx + jnp.log(z)
        aux_ref[3:4, :] = jnp.where(runner_up == m, 1.0, 0.0)
        aux_ref[4:8, :] = jnp.zeros((4, tt), F32)


def _topk_values_body(nk, tt, pq_ref, keys_ref, st_ref, aux_ref, topv, dup, cand):
    p = pl.program_id(1) % 2
    s = _dot_nt(keys_ref[...].astype(BF16), pq_ref[...].astype(BF16))
    st_ref[...] = s
    x = s
    for a in range(P_TOPK + 1):
        m = jnp.max(x, axis=0, keepdims=True)
        topv[p, a:a + 1, :] = m
        x = jnp.where(x == m, -jnp.inf, x)
    left = jnp.sum(jnp.where(x > -jnp.inf, 1.0, 0.0), axis=0, keepdims=True)
    dup[p, 0:1, :] =jnp.where(left == float(nk - P_TOPK - 1), 0.0, 1.0)

    @pl.when(p == 1)
    def _():
        pairs = _peer_pairs()
        npad = cand.shape[0]
        cand[len(pairs):npad, :] = jnp.full((npad - len(pairs), tt), -jnp.inf, F32)
        for r, (a, b) in enumerate(pairs):
            cand[r:r + 1, :] = topv[0, a:a + 1, :] + topv[1, b:b + 1, :]
        x = cand[...]
        mx = None
        z = None
        for a in range(P_TOPK):
            m = jnp.max(x, axis=0, keepdims=True)
            if a == 0:
                mx = m
                z = jnp.ones_like(m)
            else:
                z = z + jnp.exp(m - mx)
            x = jnp.where(x == m, -jnp.inf, x)
        left = jnp.sum(jnp.where(x > -jnp.inf, 1.0, 0.0), axis=0, keepdims=True)
        merged = jnp.where(left == float(len(pairs) - P_TOPK), 0.0, 1.0)
        aux_ref[0:1, :] = m
        aux_ref[1:2, :] = jnp.zeros((1, tt), F32)
        aux_ref[2:3, :] = mx + jnp.log(z)
        aux_ref[3:4, :] = jnp.maximum(merged, jnp.maximum(dup[0, 0:1, :], dup[1, 0:1, :]))
        aux_ref[4:8, :] = jnp.zeros((4, tt), F32)


def _peer_topk(pq, keys4, layer, tt, exact):
    n, qw = pq.shape
    hp, nk, half = keys4.shape[1:]
    heads = hp // 2
    npairs = -(-len(_peer_pairs()) // SUBLANES) * SUBLANES
    ntop = -(-(P_TOPK + 1) // SUBLANES) * SUBLANES
    if exact:
        body = functools.partial(_topk_body, nk, tt)
        scratch = [pltpu.VMEM((2, ntop, tt), F32), pltpu.VMEM((2, ntop, tt), F32),
                   pltpu.VMEM((npairs, tt), F32), pltpu.VMEM((npairs, tt), F32)]
    else:
        body = functools.partial(_topk_values_body, nk, tt)
        scratch = [pltpu.VMEM((2, ntop, tt), F32), pltpu.VMEM((2, SUBLANES, tt), F32),
                   pltpu.VMEM((npairs, tt), F32)]
    return tuple(pl.pallas_call(
        body,
        grid=(n // tt, hp),
        in_specs=[pl.BlockSpec((tt, half), lambda i, k: (i, k)),
                  pl.BlockSpec((None, None, nk, half), lambda i, k: (layer, k, 0, 0))],
        out_specs=[pl.BlockSpec((None, nk, tt), lambda i, k: (k, 0, i)),
                   pl.BlockSpec((None, SUBLANES, tt), lambda i, k: (k // 2, 0, i))],
        out_shape=[jax.ShapeDtypeStruct((hp, nk, n), F32),
                   jax.ShapeDtypeStruct((heads, SUBLANES, n), F32)],
        scratch_shapes=scratch,
        compiler_params=_cparams(2),
        name="peer_topk" if exact else "peer_topk_values",
    )(pq, keys4))


def _peer_body(heads, nk, tt, ec, tie_ref, h2_ref, st_ref, aux_ref, u_ref, v_ref, o_ref, at_scr, w_scr, p_scr):
    j = pl.program_id(1)
    rt = 4 * SUBLANES
    rows = lax.broadcasted_iota(jnp.int32, (rt, LANES), 0)
    per = ec // nk
    grp = SUBLANES // per
    base = pl.multiple_of((j // grp) * SUBLANES, SUBLANES)
    sub = j % grp

    def weights(exact_ties):
        for ii in range(per):
            i_glob = j * per + ii
            for lb in range(tt // LANES):
                ln = slice(lb * LANES, (lb + 1) * LANES)
                for r in range(nk // rt):
                    wacc = jnp.zeros((rt, LANES), F32)
                    for h in range(heads):
                        tile = st_ref[2 * h, pl.ds(base, SUBLANES), ln]
                        s0b = tile[ii:ii + 1]
                        for k in range(1, grp):
                            s0b = jnp.where(sub == k, tile[k * per + ii:k * per + ii + 1], s0b)
                        cnd = st_ref[2 * h + 1, r * rt:(r + 1) * rt, ln] + s0b
                        thr = aux_ref[h, 0:1, ln]
                        c0 = aux_ref[h, 2:3, ln]
                        if exact_ties:
                            eio = (i_glob * nk + r * rt + rows).astype(F32)
                            sel = (cnd > thr) | ((cnd == thr) & (eio <= aux_ref[h, 1:2, ln]))
                        else:
                            sel = cnd >= thr
                        wacc = wacc + jnp.where(sel, jnp.exp(cnd - c0), 0.0)
                    w_scr[ii * nk + r * rt:ii * nk + (r + 1) * rt, ln] = wacc

    tie = tie_ref[pl.program_id(0)]

    @pl.when(tie == 0)
    def _():
        weights(False)

    @pl.when(tie != 0)
    def _():
        weights(True)

    at_scr[...] = _dot_nt(u_ref[...], h2_ref[...])
    for ii in range(per):
        for lb in range(tt // LANES):
            ln = slice(lb * LANES, (lb + 1) * LANES)
            pt = w_scr[ii * nk:(ii + 1) * nk, ln] * _gelu(at_scr[ii * nk:(ii + 1) * nk, ln])
            p_scr[ln, ii * nk:(ii + 1) * nk] = pt.T.astype(BF16)

    @pl.when(j == 0)
    def _():
        o_ref[...] = jnp.zeros(o_ref.shape, F32)

    d = o_ref.shape[1]
    cw = min(d, 1024)
    for cb in range(d // cw):
        o_ref[:, cb * cw:(cb + 1) * cw] += _dot(p_scr[...], v_ref[:, cb * cw:(cb + 1) * cw])


def _peer_dense(h2, st, aux, tie, u_tab, v_tab, layer, tt, ec):
    n, d = h2.shape
    e = u_tab.shape[1]
    hp, nk, _ = st.shape
    heads = hp // 2
    assert nk == LANES and e == nk * nk and ec % nk == 0 and tt % LANES == 0
    assert SUBLANES % (ec // nk) == 0
    return pl.pallas_call(
        functools.partial(_peer_body, heads, nk, tt, ec),
        grid_spec=pltpu.PrefetchScalarGridSpec(
            num_scalar_prefetch=1,
            grid=(n // tt, e // ec),
            in_specs=[pl.BlockSpec((tt, d), lambda i, j, f: (i, 0)),
                      pl.BlockSpec((hp, nk, tt), lambda i, j, f: (0, 0, i)),
                      pl.BlockSpec((heads, SUBLANES, tt), lambda i, j, f: (0, 0, i)),
                      pl.BlockSpec((None, ec, d), lambda i, j, f: (layer, j, 0)),
                      pl.BlockSpec((None, ec, d), lambda i, j, f: (layer, j, 0))],
            out_specs=pl.BlockSpec((tt, d), lambda i, j, f: (i, 0)),
            scratch_shapes=[pltpu.VMEM((ec, tt), F32), pltpu.VMEM((ec, tt), F32), pltpu.VMEM((tt, ec), BF16)]),
        out_shape=jax.ShapeDtypeStruct((n, d), F32),
        compiler_params=_cparams(2),
        name="peer_dense",
    )(tie, h2, st, aux, u_tab, v_tab)


def _resid_body(final, x_ref, p_ref, gate_ref, *rest):
    x = x_ref[...] + gate_ref[...] * p_ref[...]
    if final:
        gf_ref, o_ref = rest
        o_ref[...] = x * lax.rsqrt(jnp.mean(x * x, axis=-1, keepdims=True) + EPS) * gf_ref[...]
    else:
        rest[0][...] = x


def _residual(x3, p3, mod4, k_gate, tm, g_final2=None):
    b, l, d = x3.shape
    bb, r, nt, ridx = _row_tiling(b, l, tm)
    final = g_final2 is not None
    in_specs = [pl.BlockSpec((bb, r, d), lambda i: ridx(i) + (0,)),
                pl.BlockSpec((bb, r, d), lambda i: ridx(i) + (0,)),
                pl.BlockSpec((bb, None, 1, d), lambda i: (ridx(i)[0], k_gate, 0, 0))]
    args = [x3, p3, mod4]
    if final:
        in_specs.append(pl.BlockSpec((1, d), lambda i: (0, 0)))
        args.append(g_final2)
    return pl.pallas_call(
        functools.partial(_resid_body, final),
        grid=(nt,),
        in_specs=in_specs,
        out_specs=pl.BlockSpec((bb, r, d), lambda i: ridx(i) + (0,)),
        out_shape=jax.ShapeDtypeStruct((b, l, d), F32),
        compiler_params=_cparams(1),
        name="residual",
    )(*args)


def _pick(n, pref):
    t = min(n, pref)
    while n % t:
        t //= 2
    return t


def _trunk_layer(x3, mod4, l_real, conv_hist, c0, n0, m0, state_layer, lw, layer, g_final2, want_vn,
                 prev_c=None):
    b, l, d = x3.shape
    n = b * l
    heads, dk = lw["wq"].shape[1], lw["wq"].shape[2]
    qkw = heads * dk
    cols = lw["cols"]
    chunk = LANES
    ct = min(chunk, l)
    tm = _pick(n, 512)

    z, gates = _modnorm_matmul(x3, lw["g_mix"], mod4, 1, 0, lw["w_in"], layer, tm,
                               _pick(lw["w_in"].shape[2], 1024), w_small=lw["w_gate"])
    z3 = z.reshape(b, l, -1)
    gt = jnp.swapaxes(gates[:, :2 * heads].reshape(b, l, 2 * heads), 1, 2)
    if l % chunk:
        gt = jnp.pad(gt, ((0, 0), (0, 0), (0, chunk - l % chunk)))
    conv_new = z3[:, l_real - (CONV_W - 1):l_real, cols["qk"]:cols["qk"] + qkw]
    hm, c1, n1, m1 = _mlstm(z3, cols, gt, conv_hist, lw["conv_w"], lw["conv_b"], lw["wq"], lw["wk"],
                            lw["b_i"], lw["b_f"], lw["mnorm_g"], c0, n0, m0,
                            layer, state_layer, ct, chunk, l_real, prev_c)
    outs = _gmlp(z3, cols, hm, lw["ln_g"], lw["ln_b"], lw["w_s"], lw["b_st"], layer, ct, chunk,
                 want_vn, BF16 if ct % 16 == 0 else F32)
    merged = outs[0]
    vn = outs[1] if want_vn else None
    x3 = _outproj(merged.reshape(n, -1), lw["w_out"], layer, x3, mod4, 2, tm, _pick(d, 1024))

    pq, h2 = _modnorm_matmul(x3, lw["g_ffn"], mod4, 4, 3, lw["w_pq"], layer, tm,
                             _pick(lw["w_pq"].shape[2], 512), emit_h=True)
    if l_real < l:
        pq = pq.reshape(b, l, -1)[:, :l_real].reshape(b * l_real, -1)
        h2 = h2.reshape(b, l, -1)[:, :l_real].reshape(b * l_real, -1)
    nr = b * l_real
    tt = _pick(nr, 512)
    tk = _pick(nr, 1024)
    st, aux = _peer_topk(pq, lw["keys"], layer, tk, exact=False)
    st, aux = lax.cond(jnp.max(aux[:, 3, :]) > 0,
                       lambda: _peer_topk(pq, lw["keys"], layer, tk, exact=True),
                       lambda: (st, aux))
    tie =(jnp.max(aux[:, 3, :].reshape(-1, nr // tt, tt), axis=(0, 2)) > 0).astype(jnp.int32)
    peer = _peer_dense(h2, st, aux, tie, lw["u_tab"], lw["v_tab"], layer, tt,
                       _pick(lw["u_tab"].shape[1], 512))
    peer = peer.reshape(b, l_real, d)
    if l_real < l:
        peer = jnp.pad(peer, ((0, 0), (0, l - l_real), (0, 0)))
    x3 = _residual(x3, peer, mod4, 5, _pick(n, 256), g_final2)
    return x3, conv_new, c1, n1, m1[:, :, 0], vn


def kernel(x_prompt, x_sample, state_C, state_n, state_m, state_conv, c_prompt, c_sample, w_ada, b_ada, g_mix, w_in, conv_w, conv_b, wq_m, wk_m, b_i, b_f, mnorm_g, ln_g, ln_b, w_s, b_s, w_out, g_ffn, w_pq, sub_keys, u_tab, v_tab, g_final):
    depth = w_ada.shape[0]
    bp, lp, d = x_prompt.shape
    bs, ls, _ = x_sample.shape
    heads, dk = wq_m.shape[1], wq_m.shape[2]
    dv = state_C.shape[-1]
    qkw, vw = heads * dk, heads * dv
    gw = ln_g.shape[1]

    ls_pad = -(-ls // SUBLANES) * SUBLANES
    xs = jnp.pad(x_sample, ((0, 0), (0, ls_pad - ls), (0, 0)))
    xp = x_prompt

    n_c = bp + bs
    n_c_pad = -(-n_c // SUBLANES) * SUBLANES
    c_all = jnp.pad(jnp.concatenate([c_prompt, c_sample], axis=0), ((0, n_c_pad - n_c), (0, 0)))

    o_v, o_o, o_g = qkw, qkw + vw, qkw + 2 * vw
    o_zu = o_g + 2 * heads
    cols = {"v": 0, "o": vw, "zu": 2 * vw, "zv": 2 * vw + gw, "ga": 2 * vw + 2 * gw,
            "gb": 2 * vw + 2 * gw + d, "qk": 2 * vw + 2 * gw + 2 * d}

    hist_pad = SUBLANES - (CONV_W - 1)
    zero_hist = jnp.zeros((bp, SUBLANES, qkw), F32)
    zero_c = jnp.zeros((1, bp, heads, dk, dv), F32)
    zero_n = jnp.zeros((1, bp, heads, dk), F32)
    zero_m = jnp.zeros((1, bp, heads, 1), F32)
    state_m4 = state_m[..., None]
    g_final2 = g_final.reshape(1, d)
    b_ada3 = b_ada[:, None, :]

    p_out = [[], [], [], []]
    s_out = [[], [], [], [], []]
    lw = {
        "cols": cols,
        "g_mix": g_mix[:, None, :], "g_ffn": g_ffn[:, None, :],
        "w_in": jnp.concatenate([w_in[:, :, o_v:o_g].astype(BF16), w_in[:, :, o_zu:].astype(BF16),
                                 w_in[:, :, :qkw].astype(BF16)], axis=2),
        "w_gate": jnp.pad(w_in[:, :, o_g:o_zu], ((0, 0), (0, 0), (0, LANES - 2 * heads))).astype(BF16),
        "conv_w": conv_w, "conv_b": conv_b[:, None, :],
        "wq": wq_m.astype(BF16), "wk": wk_m.astype(BF16),
        "b_i": b_i[:, :, None], "b_f": b_f[:, :, None],
        "mnorm_g": mnorm_g[:, None, :], "ln_g": ln_g[:, None, :], "ln_b": ln_b[:, None, :],
        "w_s": w_s, "b_st": jnp.swapaxes(b_s, 1, 2),
        "w_out": w_out.astype(BF16), "w_pq": w_pq.astype(BF16),
        "keys": sub_keys.reshape(depth, -1, sub_keys.shape[3], sub_keys.shape[4]),
        "u_tab": u_tab.astype(BF16), "v_tab": v_tab.astype(BF16),
    }
    for l in range(depth):
        mod = _adaln(c_all, w_ada, b_ada3, l, _pick(w_ada.shape[2], 512))
        mod_p = mod[:bp].reshape(bp, N_MOD, 1, d)
        mod_s = mod[bp:n_c].reshape(bs, N_MOD, 1, d)
        last = l == depth - 1

        xp, cv, c1, n1, m1, _ = _trunk_layer(xp, mod_p, lp, zero_hist, zero_c, zero_n, zero_m, 0,
                                             lw, l, g_final2 if last else None, False)
        for acc, val in zip(p_out, (c1, n1, m1, cv)):
            acc.append(val)

        hist_s = jnp.pad(state_conv[l], ((0, 0), (hist_pad, 0), (0, 0)))
        stack_c = last and depth == 2
        xs, cv, c1, n1, m1, vn = _trunk_layer(xs, mod_s, ls, hist_s, state_C, state_n, state_m4, l,
                                              lw, l, g_final2 if last else None, True,
                                              s_out[0][0] if stack_c else None)
        if stack_c:
            sample_c = c1
        for acc, val in zip(s_out, (c1, n1, m1, cv, vn[:, :ls])):
            acc.append(val)

    s_stacked = [jnp.stack(a) for a in s_out[1:]]
    s_stacked.insert(0, sample_c if depth == 2 else jnp.stack(s_out[0]))
    return (xp, xs[:, :ls]) + tuple(jnp.stack(a) for a in p_out) + tuple(s_stacked)
```

```python
import functools

import jax
import jax.numpy as jnp
from jax import lax
from jax.experimental import pallas as pl
from jax.experimental.pallas import tpu as pltpu

F32 = jnp.float32
BF16 = jnp.bfloat16
EPS = 1e-6
NEG_BIG = -1e30
LANES = 128
SUBLANES = 8
VMEM_LIMIT = 56 * 1024 * 1024
CONV_W = 4
P_TOPK = 16
N_MOD = 6
SQRT_HALF = 0.7071067811865476


def _cparams(n_axes):
    return pltpu.CompilerParams(dimension_semantics=("arbitrary",) * n_axes,
                                vmem_limit_bytes=VMEM_LIMIT)


def _gelu(x):
    return 0.5 * x * (1.0 + lax.erf(x * SQRT_HALF))


def _sigmoid(x):
    return 1.0 / (1.0 + jnp.exp(-x))


def _log_sigmoid(x):
    return jnp.minimum(x, 0.0) - jnp.log(1.0 + jnp.exp(-jnp.abs(x)))


def _dot(a, b):
    return jnp.dot(a, b, preferred_element_type=F32)


def _dot_nt(a, b):
    return lax.dot_general(a, b, (((1,), (1,)), ((), ())), preferred_element_type=F32)


def _dot_tn(a, b):
    return lax.dot_general(a, b, (((0,), (0,)), ((), ())), preferred_element_type=F32)


def _split3(a):
    a1 = a.astype(BF16)
    r1 = a - a1.astype(F32)
    a2 = r1.astype(BF16)
    a3 = (r1 - a2.astype(F32)).astype(BF16)
    return a1, a2, a3


def _adaln_body(c_ref, w_ref, b_ref, o_ref):
    c = c_ref[...]
    a = c * _sigmoid(c)
    a1 = a.astype(BF16)
    a2 = (a - a1.astype(F32)).astype(BF16)
    w1 = w_ref[...].astype(BF16)
    o_ref[...] = _dot(a1, w1) + _dot(a2, w1) + b_ref[...]


def _adaln(c_all, w_ada, b_ada3, layer, tn):
    m, d = c_all.shape
    n = w_ada.shape[2]
    return pl.pallas_call(
        _adaln_body,
        grid=(n // tn,),
        in_specs=[pl.BlockSpec((m, d), lambda j: (0, 0)),
                  pl.BlockSpec((None, d, tn), lambda j: (layer, 0, j)),
                  pl.BlockSpec((None, 1, tn), lambda j: (layer, 0, j))],
        out_specs=pl.BlockSpec((m, tn), lambda j: (0, j)),
        out_shape=jax.ShapeDtypeStruct((m, n), F32),
        compiler_params=_cparams(1),
        name="adaln",
    )(c_all, w_ada, b_ada3)


def _regroup_body(shift, tn, b_lo, b_hi, a_ref, b_ref, o_ref):
    j = pl.program_id(2)
    shifted = (j >= b_lo) & (j < b_hi)

    @pl.when(shifted)
    def _():
        cat = jnp.concatenate([a_ref[...], b_ref[...]], axis=1)
        o_ref[...] = pltpu.roll(cat, tn + LANES - shift, axis=1)[:, :tn].astype(BF16)

    @pl.when(jnp.logical_not(shifted))
    def _():
        o_ref[...] = a_ref[...].astype(BF16)


def _regroup_w_in(w_in, qkw, vw, n_gate, tn, tr):
    depth, d, in_w = w_in.shape
    o_v, o_g = qkw, qkw + 2 * vw
    o_zu = o_g + n_gate
    assert o_v % tn == 0 and o_g % tn == 0 and (in_w - o_zu) % tn == 0 and n_gate < LANES
    n_a, n_b, n_c = (o_g - o_v) // tn, (in_w - o_zu) // tn, qkw // tn

    def a_map(l, r, j):
        src = jnp.where(j < n_a, o_v // tn + j,
                        jnp.where(j < n_a + n_b, o_g // tn + (j - n_a), j - n_a - n_b))
        return (l, r, src)

    def b_map(l, r, j):
        jj = jnp.clip(j - n_a, 0, n_b - 1)
        return (l, r, (o_g + (jj + 1) * tn) // LANES)

    return pl.pallas_call(
        functools.partial(_regroup_body, n_gate, tn, n_a, n_a + n_b),
        grid=(depth, d // tr, n_a + n_b + n_c),
        in_specs=[pl.BlockSpec((None, tr, tn), a_map), pl.BlockSpec((None, tr, LANES), b_map)],
        out_specs=pl.BlockSpec((None, tr, tn), lambda l, r, j: (l, r, j)),
        out_shape=jax.ShapeDtypeStruct((depth, d, (n_a + n_b + n_c) * tn), BF16),
        compiler_params=_cparams(3),
        name="regroup_w_in",
    )(w_in, w_in)


def _row_tiling(b, l, tm):
    if l >= tm:
        assert l % tm == 0
        per = l // tm
        return 1, tm, b * per, (lambda i: (i // per, i % per))
    assert tm % l == 0 and b % (tm // l) == 0
    bb = tm // l
    return bb, l, b // bb, (lambda i: (i, 0))


def _modnorm_body(has_small, emit_h, tm, x_ref, g_ref, sc_ref, sh_ref, w_ref, *rest):
    rest = list(rest)
    ws_ref = rest.pop(0) if has_small else None
    o_ref = rest.pop(0)
    os_ref = rest.pop(0) if has_small else None
    ho_ref = rest.pop(0) if emit_h else None
    h_scr = rest.pop(0)

    @pl.when(pl.program_id(1) == 0)
    def _():
        bb, r, d = x_ref.shape
        nsub = 4
        while bb > 1 and bb % nsub:
            nsub //= 2
        rows = tm // nsub

        def sub(k, carry):
            if bb == 1:
                sl = (slice(None), pl.ds(pl.multiple_of(k * rows, rows), rows), slice(None))
                sc, sh = sc_ref[...], sh_ref[...]
            else:
                bsub = pl.ds(pl.multiple_of(k * (bb // nsub), bb // nsub), bb // nsub)
                sl = (bsub, slice(None), slice(None))
                sc, sh = sc_ref[bsub], sh_ref[bsub]
            x = x_ref[sl]
            var = jnp.mean(x * x, axis=-1, keepdims=True)
            y = x * lax.rsqrt(var + EPS) * g_ref[...]
            h = y * (1.0 + sc) + sh
            h_scr[pl.ds(pl.multiple_of(k * rows, rows), rows), :] = h.reshape(rows, d).astype(BF16)
            return carry

        lax.fori_loop(0, nsub, sub, 0)
        if has_small:
            os_ref[...] = _dot(h_scr[...], ws_ref[...])
        if emit_h:
            ho_ref[...] = h_scr[...]

    o_ref[...] = _dot(h_scr[...], w_ref[...])


def _modnorm_matmul(x3, g3, mod4, k_scale, k_shift, w, layer, tm, tn, w_small=None, emit_h=False):
    b, l, d = x3.shape
    nout = w.shape[2]
    bb, r, nt, ridx = _row_tiling(b, l, tm)
    has_small = w_small is not None

    def xmap(i, j):
        bi, ri = ridx(i)
        return (bi, ri, 0)

    def modmap(k):
        def f(i, j):
            return (ridx(i)[0], k, 0, 0)
        return f

    in_specs = [pl.BlockSpec((bb, r, d), xmap),
                pl.BlockSpec((None, 1, d), lambda i, j: (layer, 0, 0)),
                pl.BlockSpec((bb, None, 1, d), modmap(k_scale)),
                pl.BlockSpec((bb, None, 1, d), modmap(k_shift)),
                pl.BlockSpec((None, d, tn), lambda i, j: (layer, 0, j))]
    args = [x3, g3, mod4, mod4, w]
    out_specs = [pl.BlockSpec((tm, tn), lambda i, j: (i, j))]
    out_shape = [jax.ShapeDtypeStruct((b * l, nout), F32)]
    if has_small:
        ns = w_small.shape[2]
        in_specs.append(pl.BlockSpec((None, d, ns), lambda i, j: (layer, 0, 0)))
        args.append(w_small)
        out_specs.append(pl.BlockSpec((tm, ns), lambda i, j: (i, 0)))
        out_shape.append(jax.ShapeDtypeStruct((b * l, ns), F32))
    if emit_h:
        out_specs.append(pl.BlockSpec((tm, d), lambda i, j: (i, 0)))
        out_shape.append(jax.ShapeDtypeStruct((b * l, d), BF16))
    return pl.pallas_call(
        functools.partial(_modnorm_body, has_small, emit_h, tm),
        grid=(nt, nout // tn),
        in_specs=in_specs,
        out_specs=out_specs,
        out_shape=out_shape,
        scratch_shapes=[pltpu.VMEM((tm, d), BF16)],
        compiler_params=_cparams(2),
        name="modnorm_matmul",
    )(*args)


def _mlstm_body(ct, cs, l_real, heads, dk, dv, stacked,
                qk_ref, v_ref, o_ref, gt_ref, cb_ref, cw_ref, cbias_ref, wq_ref, wk_ref,
                bi_ref, bf_ref, mg_ref, c0_ref, n0_ref, m0_ref, *rest):
    if stacked:
        prev_ref, hm_ref, call_ref, n_ref, m_ref, xbuf = rest
        c_ref = call_ref.at[1]
    else:
        hm_ref, c_ref, n_ref, m_ref, xbuf = rest
    hist = SUBLANES

    @pl.when(pl.program_id(1) == 0)
    def _():
        if stacked:
            call_ref[0] = prev_ref[...]
        c_ref[...] = c0_ref[...]
        n_ref[...] = n0_ref[...]
        m_ref[...] = jnp.broadcast_to(m0_ref[...], m_ref.shape)
        xbuf[0:hist, :] = cb_ref[...]

    xbuf[hist:hist + ct, :] = qk_ref[...]
    acc = jnp.broadcast_to(cbias_ref[...], (ct, heads * dk))
    for j in range(CONV_W):
        acc = acc + xbuf[pl.ds(hist - (CONV_W - 1) + j, ct), :] * cw_ref[j:j + 1, :]
    xc = (acc * _sigmoid(acc)).astype(BF16)
    xbuf[0:hist, :] = xbuf[ct:ct + hist, :]

    g = gt_ref[...]
    ig_all = g[0:heads] + bi_ref[...]
    lf_all = _log_sigmoid(g[heads:2 * heads] + bf_ref[...])
    if l_real < cs:
        valid = lax.broadcasted_iota(jnp.int32, (heads, cs), 1) < l_real
        ig_all = jnp.where(valid, ig_all, NEG_BIG)
        lf_all = jnp.where(valid, lf_all, 0.0)
    upper = (lax.broadcasted_iota(jnp.int32, (cs, cs), 0)
             <= lax.broadcasted_iota(jnp.int32, (cs, cs), 1)).astype(BF16)
    l1, l2, l3 = _split3(lf_all)
    f_all = _dot(l1, upper) + _dot(l2, upper) + _dot(l3, upper)

    tt = lax.broadcasted_iota(jnp.int32, (ct, cs), 0)
    ss = lax.broadcasted_iota(jnp.int32, (ct, cs), 1)
    diag = tt == ss
    tril = ss <= tt
    pad_rows = cs - ct

    m_all = m_ref[...]
    n_all = n_ref[...]

    def gates_and_projections(h):
        ig_r = ig_all[h:h + 1, :]
        f_r = f_all[h:h + 1, :]
        m0 = m_all[h:h + 1, 0:1]
        f_c = jnp.sum(jnp.where(diag, f_r, 0.0), axis=1, keepdims=True)
        ig_c = jnp.sum(jnp.where(diag, ig_r, 0.0), axis=1, keepdims=True)
        cm_c = jnp.max(jnp.where(tril, ig_r - f_r, -jnp.inf), axis=1, keepdims=True)
        m_c = f_c + jnp.maximum(m0, cm_c)
        dmat = jnp.exp(jnp.where(tril, f_c - f_r + ig_r - m_c, -jnp.inf))
        a_c = jnp.exp(f_c + m0 - m_c)
        xh = xc[:, h * dk:(h + 1) * dk]
        q = _dot(xh, wq_ref[h])
        k = _dot(xh, wk_ref[h]) * (dk ** -0.5)
        v = v_ref[:, h * dv:(h + 1) * dv]
        m_l = m_c[ct - 1:ct, :]
        f_l = f_c[ct - 1:ct, :]
        ws_c = jnp.exp(f_l - f_c + ig_c - m_l)
        a_l = jnp.exp(f_l + m0 - m_l)
        kw = k * ws_c
        if pad_rows:
            k_p = jnp.concatenate([k, jnp.zeros((pad_rows, dk), F32)], axis=0)
            kw_p = jnp.concatenate([kw, jnp.zeros((pad_rows, dk), F32)], axis=0)
            v_p = jnp.concatenate([v, jnp.zeros((pad_rows, dv), F32)], axis=0)
        else:
            k_p, kw_p, v_p = k, kw, v
        return dict(q=q, q_b=q.astype(BF16), k_b=k_p.astype(BF16), kw=kw, kw_b=kw_p.astype(BF16),
                    v_b=v_p.astype(BF16), dmat=dmat, a_c=a_c, a_l=a_l, m_c=m_c, m_l=m_l)

    def first_matmuls(h, t):
        c_old = c_ref[h]
        t["s"] = _dot_nt(t["q_b"], t["k_b"]) * t["dmat"]
        t["qc"] = _dot(t["q_b"], c_old.astype(BF16))
        return t

    def update_memory(h, t):
        c_ref[h] = t["a_l"] * c_ref[h] + _dot_tn(t["kw_b"], t["v_b"])

    def outputs(h, t):
        n_old = n_all[h:h + 1, :]
        s, a_c = t["s"], t["a_c"]
        num = a_c * t["qc"] + _dot(s.astype(BF16), t["v_b"])
        dotn = (a_c * jnp.sum(t["q"] * n_old, axis=1, keepdims=True)
                + jnp.sum(s, axis=1, keepdims=True))
        den = jnp.maximum(jnp.abs(dotn), jnp.exp(-t["m_c"])) + EPS
        hh = num / den
        hn = hh * lax.rsqrt(jnp.mean(hh * hh, axis=1, keepdims=True) + EPS)
        hn = hn * mg_ref[:, h * dv:(h + 1) * dv]
        hm_ref[:, h * dv:(h + 1) * dv] = hn * _sigmoid(o_ref[:, h * dv:(h + 1) * dv])
        n_new = t["a_l"] * n_old + jnp.sum(t["kw"], axis=0, keepdims=True)
        return n_new, jnp.broadcast_to(t["m_l"], (1, m_ref.shape[1]))

    if pad_rows:
        ts = [gates_and_projections(h) for h in range(heads)]
        ts = [first_matmuls(h, ts[h]) for h in range(heads)]
        for h in range(heads):
            update_memory(h, ts[h])
        new = [outputs(h, ts[h]) for h in range(heads)]
    else:
        new = []
        for h in range(heads):
            t = first_matmuls(h, gates_and_projections(h))
            new.append(outputs(h, t))
            update_memory(h, t)

    for h in range(heads):
        n_ref[h:h + 1, :] = new[h][0]
        m_ref[h:h + 1, :] = new[h][1]


def _mlstm(z3, cols, gt, conv_hist, conv_w, conv_b3, wq, wk, bi3, bf3, mg3, c0, n0, m0,
           layer, state_layer, ct, cs, l_real, prev_c=None):
    b, l, _ = z3.shape
    heads, dk = wq.shape[1], wq.shape[2]
    dv = c0.shape[-1]
    qkw, vw = heads * dk, heads * dv
    nc = l // ct
    stacked = prev_c is not None
    assert cols["qk"] % qkw == 0 and cols["v"] % vw == 0 and cols["o"] % vw == 0
    body = functools.partial(_mlstm_body, ct, cs, l_real, heads, dk, dv, stacked)
    in_specs = [
        pl.BlockSpec((None, ct, qkw), lambda i, c: (i, c, cols["qk"] // qkw)),
        pl.BlockSpec((None, ct, vw), lambda i, c: (i, c, cols["v"] // vw)),
        pl.BlockSpec((None, ct, vw), lambda i, c: (i, c, cols["o"] // vw)),
        pl.BlockSpec((None, 2 * heads, cs), lambda i, c: (i, 0, c)),
        pl.BlockSpec((None, SUBLANES, qkw), lambda i, c: (i, 0, 0)),
        pl.BlockSpec((None, CONV_W, qkw), lambda i, c: (layer, 0, 0)),
        pl.BlockSpec((None, 1, qkw), lambda i, c: (layer, 0, 0)),
        pl.BlockSpec((None, heads, dk, dk), lambda i, c: (layer, 0, 0, 0)),
        pl.BlockSpec((None, heads, dk, dk), lambda i, c: (layer, 0, 0, 0)),
        pl.BlockSpec((None, heads, 1), lambda i, c: (layer, 0, 0)),
        pl.BlockSpec((None, heads, 1), lambda i, c: (layer, 0, 0)),
        pl.BlockSpec((None, 1, vw), lambda i, c: (layer, 0, 0)),
        pl.BlockSpec((None, None, heads, dk, dv), lambda i, c: (state_layer, i, 0, 0, 0)),
        pl.BlockSpec((None, None, heads, dk), lambda i, c: (state_layer, i, 0, 0)),
        pl.BlockSpec((None, None, heads, 1), lambda i, c: (state_layer, i, 0, 0)),
    ]
    out_specs = [
        pl.BlockSpec((None, ct, vw), lambda i, c: (i, c, 0)),
        pl.BlockSpec((None, heads, dk, dv), lambda i, c: (i, 0, 0, 0)),
        pl.BlockSpec((None, heads, dk), lambda i, c: (i, 0, 0)),
        pl.BlockSpec((None, heads, LANES), lambda i, c: (i, 0, 0)),
    ]
    out_shape = [
        jax.ShapeDtypeStruct((b, l, vw), F32),
        jax.ShapeDtypeStruct((b, heads, dk, dv), F32),
        jax.ShapeDtypeStruct((b, heads, dk), F32),
        jax.ShapeDtypeStruct((b, heads, LANES), F32),
    ]
    args = [z3, z3, z3, gt, conv_hist, conv_w, conv_b3, wq, wk, bi3, bf3, mg3, c0, n0, m0]
    if stacked:
        in_specs.append(pl.BlockSpec((None, heads, dk, dv), lambda i, c: (i, 0, 0, 0)))
        args.append(prev_c)
        out_specs[1] = pl.BlockSpec((2, None, heads, dk, dv), lambda i, c: (0, i, 0, 0, 0))
        out_shape[1] = jax.ShapeDtypeStruct((2, b, heads, dk, dv), F32)
    return pl.pallas_call(
        body,
        grid=(b, nc),
        in_specs=in_specs,
        out_specs=out_specs,
        out_shape=out_shape,
        scratch_shapes=[pltpu.VMEM((ct + SUBLANES, qkw), F32)],
        compiler_params=_cparams(2),
        name="mlstm",
    )(*args)


def _gmlp_body(ct, cs, groups, gd, emit_vn, zu_ref, zv_ref, ga_ref, gb_ref, hm_ref, lng_ref, lnb_ref,
               ws_ref, bst_ref, *outs):
    mg_ref = outs[0]
    gv = _gelu(zv_ref[...])
    mu = jnp.mean(gv, axis=-1, keepdims=True)
    xc = gv - mu
    var = jnp.mean(xc * xc, axis=-1, keepdims=True)
    vn = xc * lax.rsqrt(var + EPS) * lng_ref[...] + lnb_ref[...]
    if emit_vn:
        outs[1][...] = vn
    vn_b = vn.astype(BF16)
    if cs > ct:
        vn_b = jnp.concatenate([vn_b.astype(F32), jnp.zeros((cs - ct, vn.shape[1]), F32)], axis=0).astype(BF16)
    tril = (lax.broadcasted_iota(jnp.int32, (ct, cs), 1) <= lax.broadcasted_iota(jnp.int32, (ct, cs), 0))
    for g in range(groups):
        sl = slice(g * gd, (g + 1) * gd)
        w = jnp.where(tril, ws_ref[g, 0:ct, :], 0.0).astype(BF16)
        f = _dot(w, vn_b[:, sl]) + bst_ref[0:ct, g:g + 1]
        sg = _gelu(zu_ref[:, sl]) * f
        merged = _sigmoid(ga_ref[:, sl]) * hm_ref[:, sl] + _sigmoid(gb_ref[:, sl]) * sg
        mg_ref[:, sl] = merged.astype(mg_ref.dtype)


def _gmlp(z3, cols, hm, lng3, lnb3, w_s, b_st, layer, ct, cs, emit_vn, merged_dtype):
    b, l, _ = z3.shape
    gw = hm.shape[-1]
    groups = w_s.shape[1]
    gd = gw // groups
    nc = l // ct

    def zspec(name):
        assert cols[name] % gw == 0
        return pl.BlockSpec((None, ct, gw), lambda i, c: (i, c, cols[name] // gw))

    in_specs = [zspec("zu"), zspec("zv"), zspec("ga"), zspec("gb"),
                pl.BlockSpec((None, ct, gw), lambda i, c: (i, c, 0)),
                pl.BlockSpec((None, 1, gw), lambda i, c: (layer, 0, 0)),
                pl.BlockSpec((None, 1, gw), lambda i, c: (layer, 0, 0)),
                pl.BlockSpec((None, groups, cs, cs), lambda i, c: (layer, 0, 0, 0)),
                pl.BlockSpec((None, cs, groups), lambda i, c: (layer, 0, 0))]
    out_specs = [pl.BlockSpec((None, ct, gw), lambda i, c: (i, c, 0))]
    out_shape = [jax.ShapeDtypeStruct((b, l, gw), merged_dtype)]
    if emit_vn:
        out_specs.append(pl.BlockSpec((None, ct, gw), lambda i, c: (i, c, 0)))
        out_shape.append(jax.ShapeDtypeStruct((b, l, gw), F32))
    return pl.pallas_call(
        functools.partial(_gmlp_body, ct, cs, groups, gd, emit_vn),
        grid=(b, nc),
        in_specs=in_specs,
        out_specs=out_specs,
        out_shape=out_shape,
        compiler_params=_cparams(2),
        name="gmlp_merge",
    )(z3, z3, z3, z3, hm, lng3, lnb3, w_s, b_st)


def _outproj_body(a_ref, w_ref, x_ref, gate_ref, o_ref):
    acc = _dot(a_ref[...].astype(BF16), w_ref[...])
    o_ref[...] = x_ref[...] + gate_ref[...] * acc.reshape(x_ref.shape)


def _outproj(merged2, w, layer, x3, mod4, k_gate, tm, tn):
    b, l, d = x3.shape
    bb, r, nt, ridx = _row_tiling(b, l, tm)
    kdim = merged2.shape[1]
    return pl.pallas_call(
        _outproj_body,
        grid=(nt, d // tn),
        in_specs=[pl.BlockSpec((tm, kdim), lambda i, j: (i, 0)),
                  pl.BlockSpec((None, kdim, tn), lambda i, j: (layer, 0, j)),
                  pl.BlockSpec((bb, r, tn), lambda i, j: ridx(i) + (j,)),
                  pl.BlockSpec((bb, None, 1, tn), lambda i, j: (ridx(i)[0], k_gate, 0, j))],
        out_specs=pl.BlockSpec((bb, r, tn), lambda i, j: ridx(i) + (j,)),
        out_shape=jax.ShapeDtypeStruct((b, l, d), F32),
        compiler_params=_cparams(2),
        name="outproj_residual",
    )(merged2, w, x3, mod4)


def _peer_pairs():
    k = P_TOPK + 1
    return [(a, b) for a in range(k) for b in range(k) if (a + 1) * (b + 1) <= k]


def _topk_body(nk, tt, pq_ref, keys_ref, st_ref, aux_ref, topv, topi, cand, cid):
    p = pl.program_id(1) % 2
    riota = lax.broadcasted_iota(jnp.int32, (nk, tt), 0).astype(F32)
    s = _dot_nt(keys_ref[...].astype(BF16), pq_ref[...].astype(BF16))
    st_ref[...] = s
    x = s
    for a in range(P_TOPK + 1):
        m = jnp.max(x, axis=0, keepdims=True)
        idx = jnp.min(jnp.where(x == m, riota, float(nk)), axis=0, keepdims=True)
        topv[p, a:a + 1, :] = m
        topi[p, a:a + 1, :] = idx
        x = jnp.where(riota == idx, -jnp.inf, x)

    @pl.when(p == 1)
    def _():
        pairs = _peer_pairs()
        npad = cand.shape[0]
        cand[len(pairs):npad, :] = jnp.full((npad - len(pairs), tt), -jnp.inf, F32)
        cid[len(pairs):npad, :] = jnp.full((npad - len(pairs), tt), 0.0, F32)
        for r, (a, b) in enumerate(pairs):
            cand[r:r + 1, :] = topv[0, a:a + 1, :] + topv[1, b:b + 1, :]
            cid[r:r + 1, :] = topi[0, a:a + 1, :] * float(nk) + topi[1, b:b + 1, :]
        x = cand[...]
        ids = cid[...]
        big = float(nk * nk)
        mx = None
        z = None
        for a in range(P_TOPK):
            m = jnp.max(x, axis=0, keepdims=True)
            e = jnp.min(jnp.where(x == m, ids, big), axis=0, keepdims=True)
            if a == 0:
                mx = m
                z = jnp.ones_like(m)
            else:
                z = z + jnp.exp(m - mx)
            x = jnp.where(ids == e, -jnp.inf, x)
        runner_up = jnp.max(x, axis=0, keepdims=True)
        aux_ref[0:1, :] = m
        aux_ref[1:2, :] = e
        aux_ref[2:3, :] = mx + jnp.log(z)
        aux_ref[3:4, :] = jnp.where(runner_up == m, 1.0, 0.0)
        aux_ref[4:8, :] = jnp.zeros((4, tt), F32)


def _peer_topk(pq, keys4, layer, tt):
    n, qw = pq.shape
    hp, nk, half = keys4.shape[1:]
    heads = hp // 2
    npairs = -(-len(_peer_pairs()) // SUBLANES) * SUBLANES
    ntop = -(-(P_TOPK + 1) // SUBLANES) * SUBLANES
    return tuple(pl.pallas_call(
        functools.partial(_topk_body, nk, tt),
        grid=(n // tt, hp),
        in_specs=[pl.BlockSpec((tt, half), lambda i, k: (i, k)),
                  pl.BlockSpec((None, None, nk, half), lambda i, k: (layer, k, 0, 0))],
        out_specs=[pl.BlockSpec((None, nk, tt), lambda i, k: (k, 0, i)),
                   pl.BlockSpec((None, SUBLANES, tt), lambda i, k: (k // 2, 0, i))],
        out_shape=[jax.ShapeDtypeStruct((hp, nk, n), F32),
                   jax.ShapeDtypeStruct((heads, SUBLANES, n), F32)],
        scratch_shapes=[pltpu.VMEM((2, ntop, tt), F32), pltpu.VMEM((2, ntop, tt), F32),
                        pltpu.VMEM((npairs, tt), F32), pltpu.VMEM((npairs, tt), F32)],
        compiler_params=_cparams(2),
        name="peer_topk",
    )(pq, keys4))


def _peer_body(heads, nk, tt, ec, tie_ref, h2_ref, st_ref, aux_ref, u_ref, v_ref, o_ref, at_scr, w_scr, p_scr):
    j = pl.program_id(1)
    rt = 4 * SUBLANES
    rows = lax.broadcasted_iota(jnp.int32, (rt, LANES), 0)
    per = ec // nk
    grp = SUBLANES // per
    base = pl.multiple_of((j // grp) * SUBLANES, SUBLANES)
    sub = j % grp

    def weights(exact_ties):
        for ii in range(per):
            i_glob = j * per + ii
            for lb in range(tt // LANES):
                ln = slice(lb * LANES, (lb + 1) * LANES)
                for r in range(nk // rt):
                    wacc = jnp.zeros((rt, LANES), F32)
                    for h in range(heads):
                        tile = st_ref[2 * h, pl.ds(base, SUBLANES), ln]
                        s0b = tile[ii:ii + 1]
                        for k in range(1, grp):
                            s0b = jnp.where(sub == k, tile[k * per + ii:k * per + ii + 1], s0b)
                        cnd = st_ref[2 * h + 1, r * rt:(r + 1) * rt, ln] + s0b
                        thr = aux_ref[h, 0:1, ln]
                        c0 = aux_ref[h, 2:3, ln]
                        if exact_ties:
                            eio = (i_glob * nk + r * rt + rows).astype(F32)
                            sel = (cnd > thr) | ((cnd == thr) & (eio <= aux_ref[h, 1:2, ln]))
                        else:
                            sel = cnd >= thr
                        wacc = wacc + jnp.where(sel, jnp.exp(cnd - c0), 0.0)
                    w_scr[ii * nk + r * rt:ii * nk + (r + 1) * rt, ln] = wacc

    tie = tie_ref[pl.program_id(0)]

    @pl.when(tie == 0)
    def _():
        weights(False)

    @pl.when(tie != 0)
    def _():
        weights(True)

    at_scr[...] = _dot_nt(u_ref[...], h2_ref[...])
    for ii in range(per):
        for lb in range(tt // LANES):
            ln = slice(lb * LANES, (lb + 1) * LANES)
            pt = w_scr[ii * nk:(ii + 1) * nk, ln] * _gelu(at_scr[ii * nk:(ii + 1) * nk, ln])
            p_scr[ln, ii * nk:(ii + 1) * nk] = pt.T.astype(BF16)

    @pl.when(j == 0)
    def _():
        o_ref[...] = jnp.zeros(o_ref.shape, F32)

    d = o_ref.shape[1]
    cw = min(d, 1024)
    for cb in range(d // cw):
        o_ref[:, cb * cw:(cb + 1) * cw] += _dot(p_scr[...], v_ref[:, cb * cw:(cb + 1) * cw])


def _peer_dense(h2, st, aux, tie, u_tab, v_tab, layer, tt, ec):
    n, d = h2.shape
    e = u_tab.shape[1]
    hp, nk, _ = st.shape
    heads = hp // 2
    assert nk == LANES and e == nk * nk and ec % nk == 0 and tt % LANES == 0
    assert SUBLANES % (ec // nk) == 0
    return pl.pallas_call(
        functools.partial(_peer_body, heads, nk, tt, ec),
        grid_spec=pltpu.PrefetchScalarGridSpec(
            num_scalar_prefetch=1,
            grid=(n // tt, e // ec),
            in_specs=[pl.BlockSpec((tt, d), lambda i, j, f: (i, 0)),
                      pl.BlockSpec((hp, nk, tt), lambda i, j, f: (0, 0, i)),
                      pl.BlockSpec((heads, SUBLANES, tt), lambda i, j, f: (0, 0, i)),
                      pl.BlockSpec((None, ec, d), lambda i, j, f: (layer, j, 0)),
                      pl.BlockSpec((None, ec, d), lambda i, j, f: (layer, j, 0))],
            out_specs=pl.BlockSpec((tt, d), lambda i, j, f: (i, 0)),
            scratch_shapes=[pltpu.VMEM((ec, tt), F32), pltpu.VMEM((ec, tt), F32), pltpu.VMEM((tt, ec), BF16)]),
        out_shape=jax.ShapeDtypeStruct((n, d), F32),
        compiler_params=_cparams(2),
        name="peer_dense",
    )(tie, h2, st, aux, u_tab, v_tab)


def _resid_body(final, x_ref, p_ref, gate_ref, *rest):
    x = x_ref[...] + gate_ref[...] * p_ref[...]
    if final:
        gf_ref, o_ref = rest
        o_ref[...] = x * lax.rsqrt(jnp.mean(x * x, axis=-1, keepdims=True) + EPS) * gf_ref[...]
    else:
        rest[0][...] = x


def _residual(x3, p3, mod4, k_gate, tm, g_final2=None):
    b, l, d = x3.shape
    bb, r, nt, ridx = _row_tiling(b, l, tm)
    final = g_final2 is not None
    in_specs = [pl.BlockSpec((bb, r, d), lambda i: ridx(i) + (0,)),
                pl.BlockSpec((bb, r, d), lambda i: ridx(i) + (0,)),
                pl.BlockSpec((bb, None, 1, d), lambda i: (ridx(i)[0], k_gate, 0, 0))]
    args = [x3, p3, mod4]
    if final:
        in_specs.append(pl.BlockSpec((1, d), lambda i: (0, 0)))
        args.append(g_final2)
    return pl.pallas_call(
        functools.partial(_resid_body, final),
        grid=(nt,),
        in_specs=in_specs,
        out_specs=pl.BlockSpec((bb, r, d), lambda i: ridx(i) + (0,)),
        out_shape=jax.ShapeDtypeStruct((b, l, d), F32),
        compiler_params=_cparams(1),
        name="residual",
    )(*args)


def _pick(n, pref):
    t = min(n, pref)
    while n % t:
        t //= 2
    return t


def _trunk_layer(x3, mod4, l_real, conv_hist, c0, n0, m0, state_layer, lw, layer, g_final2, want_vn,
                 prev_c=None):
    b, l, d = x3.shape
    n = b * l
    heads, dk = lw["wq"].shape[1], lw["wq"].shape[2]
    qkw = heads * dk
    cols = lw["cols"]
    chunk = LANES
    ct = min(chunk, l)
    tm = _pick(n, 512)

    z, gates = _modnorm_matmul(x3, lw["g_mix"], mod4, 1, 0, lw["w_in"], layer, tm,
                               _pick(lw["w_in"].shape[2], 1024), w_small=lw["w_gate"])
    z3 = z.reshape(b, l, -1)
    gt = jnp.swapaxes(gates[:, :2 * heads].reshape(b, l, 2 * heads), 1, 2)
    if l % chunk:
        gt = jnp.pad(gt, ((0, 0), (0, 0), (0, chunk - l % chunk)))
    conv_new = z3[:, l_real - (CONV_W - 1):l_real, cols["qk"]:cols["qk"] + qkw]
    hm, c1, n1, m1 = _mlstm(z3, cols, gt, conv_hist, lw["conv_w"], lw["conv_b"], lw["wq"], lw["wk"],
                            lw["b_i"], lw["b_f"], lw["mnorm_g"], c0, n0, m0,
                            layer, state_layer, ct, chunk, l_real, prev_c)
    outs = _gmlp(z3, cols, hm, lw["ln_g"], lw["ln_b"], lw["w_s"], lw["b_st"], layer, ct, chunk,
                 want_vn, BF16 if ct % 16 == 0 else F32)
    merged = outs[0]
    vn = outs[1] if want_vn else None
    x3 = _outproj(merged.reshape(n, -1), lw["w_out"], layer, x3, mod4, 2, tm, _pick(d, 1024))

    pq, h2 = _modnorm_matmul(x3, lw["g_ffn"], mod4, 4, 3, lw["w_pq"], layer, tm,
                             _pick(lw["w_pq"].shape[2], 512), emit_h=True)
    if l_real < l:
        pq = pq.reshape(b, l, -1)[:, :l_real].reshape(b * l_real, -1)
        h2 = h2.reshape(b, l, -1)[:, :l_real].reshape(b * l_real, -1)
    nr = b * l_real
    tt = _pick(nr, 512)
    tk = _pick(nr, 1024)
    st, aux = _peer_topk(pq, lw["keys"], layer, tk)
    tie = (jnp.max(aux[:, 3, :].reshape(-1, nr // tt, tt), axis=(0, 2)) > 0).astype(jnp.int32)
    peer = _peer_dense(h2, st, aux, tie, lw["u_tab"], lw["v_tab"], layer, tt,
                       _pick(lw["u_tab"].shape[1], 512))
    peer = peer.reshape(b, l_real, d)
    if l_real < l:
        peer = jnp.pad(peer, ((0, 0), (0, l - l_real), (0, 0)))
    x3 = _residual(x3, peer, mod4, 5, _pick(n, 256), g_final2)
    return x3, conv_new, c1, n1, m1[:, :, 0], vn


def kernel(x_prompt, x_sample, state_C, state_n, state_m, state_conv, c_prompt, c_sample, w_ada, b_ada, g_mix, w_in, conv_w, conv_b, wq_m, wk_m, b_i, b_f, mnorm_g, ln_g, ln_b, w_s, b_s, w_out, g_ffn, w_pq, sub_keys, u_tab, v_tab, g_final):
    depth = w_ada.shape[0]
    bp, lp, d = x_prompt.shape
    bs, ls, _ = x_sample.shape
    heads, dk = wq_m.shape[1], wq_m.shape[2]
    dv = state_C.shape[-1]
    qkw, vw = heads * dk, heads * dv
    gw = ln_g.shape[1]

    ls_pad = -(-ls // SUBLANES) * SUBLANES
    xs = jnp.pad(x_sample, ((0, 0), (0, ls_pad - ls), (0, 0)))
    xp = x_prompt

    n_c = bp + bs
    n_c_pad = -(-n_c // SUBLANES) * SUBLANES
    c_all = jnp.pad(jnp.concatenate([c_prompt, c_sample], axis=0), ((0, n_c_pad - n_c), (0, 0)))

    o_v, o_o, o_g = qkw, qkw + vw, qkw + 2 * vw
    o_zu = o_g + 2 * heads
    cols = {"v": 0, "o": vw, "zu": 2 * vw, "zv": 2 * vw + gw, "ga": 2 * vw + 2 * gw,
            "gb": 2 * vw + 2 * gw + d, "qk": 2 * vw + 2 * gw + 2 * d}

    hist_pad = SUBLANES - (CONV_W - 1)
    zero_hist = jnp.zeros((bp, SUBLANES, qkw), F32)
    zero_c = jnp.zeros((1, bp, heads, dk, dv), F32)
    zero_n = jnp.zeros((1, bp, heads, dk), F32)
    zero_m = jnp.zeros((1, bp, heads, 1), F32)
    state_m4 = state_m[..., None]
    g_final2 = g_final.reshape(1, d)
    b_ada3 = b_ada[:, None, :]

    p_out = [[], [], [], []]
    s_out = [[], [], [], [], []]
    lw = {
        "cols": cols,
        "g_mix": g_mix[:, None, :], "g_ffn": g_ffn[:, None, :],
        "w_in": _regroup_w_in(w_in, qkw, vw, 2 * heads, _pick(qkw, 1024), _pick(d, 512)),
        "w_gate": jnp.pad(w_in[:, :, o_g:o_zu], ((0, 0), (0, 0), (0, LANES - 2 * heads))).astype(BF16),
        "conv_w": conv_w, "conv_b": conv_b[:, None, :],
        "wq": wq_m.astype(BF16), "wk": wk_m.astype(BF16),
        "b_i": b_i[:, :, None], "b_f": b_f[:, :, None],
        "mnorm_g": mnorm_g[:, None, :], "ln_g": ln_g[:, None, :], "ln_b": ln_b[:, None, :],
        "w_s": w_s, "b_st": jnp.swapaxes(b_s, 1, 2),
        "w_out": w_out.astype(BF16), "w_pq": w_pq.astype(BF16),
        "keys": sub_keys.reshape(depth, -1, sub_keys.shape[3], sub_keys.shape[4]),
        "u_tab": u_tab.astype(BF16), "v_tab": v_tab.astype(BF16),
    }
    for l in range(depth):
        mod = _adaln(c_all, w_ada, b_ada3, l, _pick(w_ada.shape[2], 512))
        mod_p = mod[:bp].reshape(bp, N_MOD, 1, d)
        mod_s = mod[bp:n_c].reshape(bs, N_MOD, 1, d)
        last = l == depth - 1

        xp, cv, c1, n1, m1, _ = _trunk_layer(xp, mod_p, lp, zero_hist, zero_c, zero_n, zero_m, 0,
                                             lw, l, g_final2 if last else None, False)
        for acc, val in zip(p_out, (c1, n1, m1, cv)):
            acc.append(val)

        hist_s = jnp.pad(state_conv[l], ((0, 0), (hist_pad, 0), (0, 0)))
        stack_c = last and depth == 2
        xs, cv, c1, n1, m1, vn = _trunk_layer(xs, mod_s, ls, hist_s, state_C, state_n, state_m4, l,
                                              lw, l, g_final2 if last else None, True,
                                              s_out[0][0] if stack_c else None)
        if stack_c:
            sample_c = c1
        for acc, val in zip(s_out, (c1, n1, m1, cv, vn[:, :ls])):
            acc.append(val)

    s_stacked = [jnp.stack(a) for a in s_out[1:]]
    s_stacked.insert(0, sample_c if depth == 2 else jnp.stack(s_out[0]))
    return (xp, xs[:, :ls]) + tuple(jnp.stack(a) for a in p_out) + tuple(s_stacked)
```

```python
import functools

import jax
import jax.numpy as jnp
from jax import lax
from jax.experimental import pallas as pl
from jax.experimental.pallas import tpu as pltpu

F32 = jnp.float32
BF16 = jnp.bfloat16
EPS = 1e-6
NEG_BIG = -1e30
LANES = 128
SUBLANES = 8
VMEM_LIMIT = 56 * 1024 * 1024
CONV_W = 4
P_TOPK = 16
N_MOD = 6
SQRT_HALF = 0.7071067811865476


def _cparams(n_axes):
    return pltpu.CompilerParams(dimension_semantics=("arbitrary",) * n_axes,
                                vmem_limit_bytes=VMEM_LIMIT)


def _gelu(x):
    return 0.5 * x * (1.0 + lax.erf(x * SQRT_HALF))


def _sigmoid(x):
    return 1.0 / (1.0 + jnp.exp(-x))


def _log_sigmoid(x):
    return jnp.minimum(x, 0.0) - jnp.log(1.0 + jnp.exp(-jnp.abs(x)))


def _dot(a, b):
    return jnp.dot(a, b, preferred_element_type=F32)


def _dot_nt(a, b):
    return lax.dot_general(a, b, (((1,), (1,)), ((), ())), preferred_element_type=F32)


def _dot_tn(a, b):
    return lax.dot_general(a, b, (((0,), (0,)), ((), ())), preferred_element_type=F32)


def _split3(a):
    a1 = a.astype(BF16)
    r1 = a - a1.astype(F32)
    a2 = r1.astype(BF16)
    a3 = (r1 - a2.astype(F32)).astype(BF16)
    return a1, a2, a3


def _adaln_body(c_ref, w_ref, b_ref, o_ref):
    c = c_ref[...]
    a = c * _sigmoid(c)
    a1 = a.astype(BF16)
    a2 = (a - a1.astype(F32)).astype(BF16)
    w1 = w_ref[...].astype(BF16)
    o_ref[...] = _dot(a1, w1) + _dot(a2, w1) + b_ref[...]


def _adaln(c_all, w_ada, b_ada3, layer, tn):
    m, d = c_all.shape
    n = w_ada.shape[2]
    return pl.pallas_call(
        _adaln_body,
        grid=(n // tn,),
        in_specs=[pl.BlockSpec((m, d), lambda j: (0, 0)),
                  pl.BlockSpec((None, d, tn), lambda j: (layer, 0, j)),
                  pl.BlockSpec((None, 1, tn), lambda j: (layer, 0, j))],
        out_specs=pl.BlockSpec((m, tn), lambda j: (0, j)),
        out_shape=jax.ShapeDtypeStruct((m, n), F32),
        compiler_params=_cparams(1),
        name="adaln",
    )(c_all, w_ada, b_ada3)


def _regroup_body(a_ref, o_ref):
    o_ref[...] = a_ref[...].T.astype(BF16)


def _regroup_w_in(w_in_t, qkw, vw, n_gate, tn, tr):
    depth, in_w, d = w_in_t.shape
    o_v, o_g = qkw, qkw + 2 * vw
    o_zu = o_g + n_gate
    assert (o_g - o_v) % tn == 0 and (in_w - o_zu) % tn == 0 and qkw % tn == 0 and o_zu % SUBLANES == 0
    n_a, n_b, n_c = (o_g - o_v) // tn, (in_w - o_zu) // tn, qkw // tn

    def src_map(l, r, j):
        row = jnp.where(j < n_a, o_v + j * tn,
                        jnp.where(j < n_a + n_b, o_zu + (j - n_a) * tn, (j - n_a - n_b) * tn))
        return (l, pl.multiple_of(row, SUBLANES), r * tr)

    return pl.pallas_call(
        _regroup_body,
        grid=(depth, d // tr, n_a + n_b + n_c),
        in_specs=[pl.BlockSpec((None, pl.Element(tn), pl.Element(tr)), src_map)],
        out_specs=pl.BlockSpec((None, tr, tn), lambda l, r, j: (l, r, j)),
        out_shape=jax.ShapeDtypeStruct((depth, d, (n_a + n_b + n_c) * tn), BF16),
        compiler_params=_cparams(3),
        name="regroup_w_in",
    )(w_in_t)


def _row_tiling(b, l, tm):
    if l >= tm:
        assert l % tm == 0
        per = l // tm
        return 1, tm, b * per, (lambda i: (i // per, i % per))
    assert tm % l == 0 and b % (tm // l) == 0
    bb = tm // l
    return bb, l, b // bb, (lambda i: (i, 0))


def _modnorm_body(has_small, emit_h, tm, x_ref, g_ref, sc_ref, sh_ref, w_ref, *rest):
    rest = list(rest)
    ws_ref = rest.pop(0) if has_small else None
    o_ref = rest.pop(0)
    os_ref = rest.pop(0) if has_small else None
    ho_ref = rest.pop(0) if emit_h else None
    h_scr = rest.pop(0)

    @pl.when(pl.program_id(1) == 0)
    def _():
        bb, r, d = x_ref.shape
        nsub = 4
        while bb > 1 and bb % nsub:
            nsub //= 2
        rows = tm // nsub

        def sub(k, carry):
            if bb == 1:
                sl = (slice(None), pl.ds(pl.multiple_of(k * rows, rows), rows), slice(None))
                sc, sh = sc_ref[...], sh_ref[...]
            else:
                bsub = pl.ds(pl.multiple_of(k * (bb // nsub), bb // nsub), bb // nsub)
                sl = (bsub, slice(None), slice(None))
                sc, sh = sc_ref[bsub], sh_ref[bsub]
            x = x_ref[sl]
            var = jnp.mean(x * x, axis=-1, keepdims=True)
            y = x * lax.rsqrt(var + EPS) * g_ref[...]
            h = y * (1.0 + sc) + sh
            h_scr[pl.ds(pl.multiple_of(k * rows, rows), rows), :] = h.reshape(rows, d).astype(BF16)
            return carry

        lax.fori_loop(0, nsub, sub, 0)
        if has_small:
            os_ref[...] = _dot(h_scr[...], ws_ref[...])
        if emit_h:
            ho_ref[...] = h_scr[...]

    o_ref[...] = _dot(h_scr[...], w_ref[...])


def _modnorm_matmul(x3, g3, mod4, k_scale, k_shift, w, layer, tm, tn, w_small=None, emit_h=False):
    b, l, d = x3.shape
    nout = w.shape[2]
    bb, r, nt, ridx = _row_tiling(b, l, tm)
    has_small = w_small is not None

    def xmap(i, j):
        bi, ri = ridx(i)
        return (bi, ri, 0)

    def modmap(k):
        def f(i, j):
            return (ridx(i)[0], k, 0, 0)
        return f

    in_specs = [pl.BlockSpec((bb, r, d), xmap),
                pl.BlockSpec((None, 1, d), lambda i, j: (layer, 0, 0)),
                pl.BlockSpec((bb, None, 1, d), modmap(k_scale)),
                pl.BlockSpec((bb, None, 1, d), modmap(k_shift)),
                pl.BlockSpec((None, d, tn), lambda i, j: (layer, 0, j))]
    args = [x3, g3, mod4, mod4, w]
    out_specs = [pl.BlockSpec((tm, tn), lambda i, j: (i, j))]
    out_shape = [jax.ShapeDtypeStruct((b * l, nout), F32)]
    if has_small:
        ns = w_small.shape[2]
        in_specs.append(pl.BlockSpec((None, d, ns), lambda i, j: (layer, 0, 0)))
        args.append(w_small)
        out_specs.append(pl.BlockSpec((tm, ns), lambda i, j: (i, 0)))
        out_shape.append(jax.ShapeDtypeStruct((b * l, ns), F32))
    if emit_h:
        out_specs.append(pl.BlockSpec((tm, d), lambda i, j: (i, 0)))
        out_shape.append(jax.ShapeDtypeStruct((b * l, d), BF16))
    return pl.pallas_call(
        functools.partial(_modnorm_body, has_small, emit_h, tm),
        grid=(nt, nout // tn),
        in_specs=in_specs,
        out_specs=out_specs,
        out_shape=out_shape,
        scratch_shapes=[pltpu.VMEM((tm, d), BF16)],
        compiler_params=_cparams(2),
        name="modnorm_matmul",
    )(*args)


def _mlstm_body(ct, cs, l_real, heads, dk, dv, stacked,
                qk_ref, v_ref, o_ref, gt_ref, cb_ref, cw_ref, cbias_ref, wq_ref, wk_ref,
                bi_ref, bf_ref, mg_ref, c0_ref, n0_ref, m0_ref, *rest):
    if stacked:
        prev_ref, hm_ref, call_ref, n_ref, m_ref, xbuf = rest
        c_ref = call_ref.at[1]
    else:
        hm_ref, c_ref, n_ref, m_ref, xbuf = rest
    hist = SUBLANES

    @pl.when(pl.program_id(1) == 0)
    def _():
        if stacked:
            call_ref[0] = prev_ref[...]
        c_ref[...] = c0_ref[...]
        n_ref[...] = n0_ref[...]
        m_ref[...] = jnp.broadcast_to(m0_ref[...], m_ref.shape)
        xbuf[0:hist, :] = cb_ref[...]

    xbuf[hist:hist + ct, :] = qk_ref[...]
    acc = jnp.broadcast_to(cbias_ref[...], (ct, heads * dk))
    for j in range(CONV_W):
        acc = acc + xbuf[pl.ds(hist - (CONV_W - 1) + j, ct), :] * cw_ref[j:j + 1, :]
    xc = (acc * _sigmoid(acc)).astype(BF16)
    xbuf[0:hist, :] = xbuf[ct:ct + hist, :]

    g = gt_ref[...]
    ig_all = g[0:heads] + bi_ref[...]
    lf_all = _log_sigmoid(g[heads:2 * heads] + bf_ref[...])
    if l_real < cs:
        valid = lax.broadcasted_iota(jnp.int32, (heads, cs), 1) < l_real
        ig_all = jnp.where(valid, ig_all, NEG_BIG)
        lf_all = jnp.where(valid, lf_all, 0.0)
    upper = (lax.broadcasted_iota(jnp.int32, (cs, cs), 0)
             <= lax.broadcasted_iota(jnp.int32, (cs, cs), 1)).astype(BF16)
    l1, l2, l3 = _split3(lf_all)
    f_all = _dot(l1, upper) + _dot(l2, upper) + _dot(l3, upper)

    tt = lax.broadcasted_iota(jnp.int32, (ct, cs), 0)
    ss = lax.broadcasted_iota(jnp.int32, (ct, cs), 1)
    diag = tt == ss
    tril = ss <= tt
    pad_rows = cs - ct

    m_all = m_ref[...]
    n_all = n_ref[...]

    def gates_and_projections(h):
        ig_r = ig_all[h:h + 1, :]
        f_r = f_all[h:h + 1, :]
        m0 = m_all[h:h + 1, 0:1]
        f_c = jnp.sum(jnp.where(diag, f_r, 0.0), axis=1, keepdims=True)
        ig_c = jnp.sum(jnp.where(diag, ig_r, 0.0), axis=1, keepdims=True)
        cm_c = jnp.max(jnp.where(tril, ig_r - f_r, -jnp.inf), axis=1, keepdims=True)
        m_c = f_c + jnp.maximum(m0, cm_c)
        dmat = jnp.exp(jnp.where(tril, f_c - f_r + ig_r - m_c, -jnp.inf))
        a_c = jnp.exp(f_c + m0 - m_c)
        xh = xc[:, h * dk:(h + 1) * dk]
        q = _dot(xh, wq_ref[h])
        k = _dot(xh, wk_ref[h]) * (dk ** -0.5)
        v = v_ref[:, h * dv:(h + 1) * dv]
        m_l = m_c[ct - 1:ct, :]
        f_l = f_c[ct - 1:ct, :]
        ws_c = jnp.exp(f_l - f_c + ig_c - m_l)
        a_l = jnp.exp(f_l + m0 - m_l)
        kw = k * ws_c
        if pad_rows:
            k_p = jnp.concatenate([k, jnp.zeros((pad_rows, dk), F32)], axis=0)
            kw_p = jnp.concatenate([kw, jnp.zeros((pad_rows, dk), F32)], axis=0)
            v_p = jnp.concatenate([v, jnp.zeros((pad_rows, dv), F32)], axis=0)
        else:
            k_p, kw_p, v_p = k, kw, v
        return dict(q=q, q_b=q.astype(BF16), k_b=k_p.astype(BF16), kw=kw, kw_b=kw_p.astype(BF16),
                    v_b=v_p.astype(BF16), dmat=dmat, a_c=a_c, a_l=a_l, m_c=m_c, m_l=m_l)

    def first_matmuls(h, t):
        c_old = c_ref[h]
        t["s"] = _dot_nt(t["q_b"], t["k_b"]) * t["dmat"]
        t["qc"] = _dot(t["q_b"], c_old.astype(BF16))
        return t

    def update_memory(h, t):
        c_ref[h] = t["a_l"] * c_ref[h] + _dot_tn(t["kw_b"], t["v_b"])

    def outputs(h, t):
        n_old = n_all[h:h + 1, :]
        s, a_c = t["s"], t["a_c"]
        num = a_c * t["qc"] + _dot(s.astype(BF16), t["v_b"])
        dotn = (a_c * jnp.sum(t["q"] * n_old, axis=1, keepdims=True)
                + jnp.sum(s, axis=1, keepdims=True))
        den = jnp.maximum(jnp.abs(dotn), jnp.exp(-t["m_c"])) + EPS
        hh = num / den
        hn = hh * lax.rsqrt(jnp.mean(hh * hh, axis=1, keepdims=True) + EPS)
        hn = hn * mg_ref[:, h * dv:(h + 1) * dv]
        hm_ref[:, h * dv:(h + 1) * dv] = hn * _sigmoid(o_ref[:, h * dv:(h + 1) * dv])
        n_new = t["a_l"] * n_old + jnp.sum(t["kw"], axis=0, keepdims=True)
        return n_new, jnp.broadcast_to(t["m_l"], (1, m_ref.shape[1]))

    if pad_rows:
        ts = [gates_and_projections(h) for h in range(heads)]
        ts = [first_matmuls(h, ts[h]) for h in range(heads)]
        for h in range(heads):
            update_memory(h, ts[h])
        new = [outputs(h, ts[h]) for h in range(heads)]
    else:
        new = []
        for h in range(heads):
            t = first_matmuls(h, gates_and_projections(h))
            new.append(outputs(h, t))
            update_memory(h, t)

    for h in range(heads):
        n_ref[h:h + 1, :] = new[h][0]
        m_ref[h:h + 1, :] = new[h][1]


def _mlstm(z3, cols, gt, conv_hist, conv_w, conv_b3, wq, wk, bi3, bf3, mg3, c0, n0, m0,
           layer, state_layer, ct, cs, l_real, prev_c=None):
    b, l, _ = z3.shape
    heads, dk = wq.shape[1], wq.shape[2]
    dv = c0.shape[-1]
    qkw, vw = heads * dk, heads * dv
    nc = l // ct
    stacked = prev_c is not None
    assert cols["qk"] % qkw == 0 and cols["v"] % vw == 0 and cols["o"] % vw == 0
    body = functools.partial(_mlstm_body, ct, cs, l_real, heads, dk, dv, stacked)
    in_specs = [
        pl.BlockSpec((None, ct, qkw), lambda i, c: (i, c, cols["qk"] // qkw)),
        pl.BlockSpec((None, ct, vw), lambda i, c: (i, c, cols["v"] // vw)),
        pl.BlockSpec((None, ct, vw), lambda i, c: (i, c, cols["o"] // vw)),
        pl.BlockSpec((None, 2 * heads, cs), lambda i, c: (i, 0, c)),
        pl.BlockSpec((None, SUBLANES, qkw), lambda i, c: (i, 0, 0)),
        pl.BlockSpec((None, CONV_W, qkw), lambda i, c: (layer, 0, 0)),
        pl.BlockSpec((None, 1, qkw), lambda i, c: (layer, 0, 0)),
        pl.BlockSpec((None, heads, dk, dk), lambda i, c: (layer, 0, 0, 0)),
        pl.BlockSpec((None, heads, dk, dk), lambda i, c: (layer, 0, 0, 0)),
        pl.BlockSpec((None, heads, 1), lambda i, c: (layer, 0, 0)),
        pl.BlockSpec((None, heads, 1), lambda i, c: (layer, 0, 0)),
        pl.BlockSpec((None, 1, vw), lambda i, c: (layer, 0, 0)),
        pl.BlockSpec((None, None, heads, dk, dv), lambda i, c: (state_layer, i, 0, 0, 0)),
        pl.BlockSpec((None, None, heads, dk), lambda i, c: (state_layer, i, 0, 0)),
        pl.BlockSpec((None, None, heads, 1), lambda i, c: (state_layer, i, 0, 0)),
    ]
    out_specs = [
        pl.BlockSpec((None, ct, vw), lambda i, c: (i, c, 0)),
        pl.BlockSpec((None, heads, dk, dv), lambda i, c: (i, 0, 0, 0)),
        pl.BlockSpec((None, heads, dk), lambda i, c: (i, 0, 0)),
        pl.BlockSpec((None, heads, LANES), lambda i, c: (i, 0, 0)),
    ]
    out_shape = [
        jax.ShapeDtypeStruct((b, l, vw), F32),
        jax.ShapeDtypeStruct((b, heads, dk, dv), F32),
        jax.ShapeDtypeStruct((b, heads, dk), F32),
        jax.ShapeDtypeStruct((b, heads, LANES), F32),
    ]
    args = [z3, z3, z3, gt, conv_hist, conv_w, conv_b3, wq, wk, bi3, bf3, mg3, c0, n0, m0]
    if stacked:
        in_specs.append(pl.BlockSpec((None, heads, dk, dv), lambda i, c: (i, 0, 0, 0)))
        args.append(prev_c)
        out_specs[1] = pl.BlockSpec((2, None, heads, dk, dv), lambda i, c: (0, i, 0, 0, 0))
        out_shape[1] = jax.ShapeDtypeStruct((2, b, heads, dk, dv), F32)
    return pl.pallas_call(
        body,
        grid=(b, nc),
        in_specs=in_specs,
        out_specs=out_specs,
        out_shape=out_shape,
        scratch_shapes=[pltpu.VMEM((ct + SUBLANES, qkw), F32)],
        compiler_params=_cparams(2),
        name="mlstm",
    )(*args)


def _gmlp_body(ct, cs, groups, gd, emit_vn, zu_ref, zv_ref, ga_ref, gb_ref, hm_ref, lng_ref, lnb_ref,
               ws_ref, bst_ref, *outs):
    mg_ref = outs[0]
    gv = _gelu(zv_ref[...])
    mu = jnp.mean(gv, axis=-1, keepdims=True)
    xc = gv - mu
    var = jnp.mean(xc * xc, axis=-1, keepdims=True)
    vn = xc * lax.rsqrt(var + EPS) * lng_ref[...] + lnb_ref[...]
    if emit_vn:
        outs[1][...] = vn
    vn_b = vn.astype(BF16)
    if cs > ct:
        vn_b = jnp.concatenate([vn_b.astype(F32), jnp.zeros((cs - ct, vn.shape[1]), F32)], axis=0).astype(BF16)
    tril = (lax.broadcasted_iota(jnp.int32, (ct, cs), 1) <= lax.broadcasted_iota(jnp.int32, (ct, cs), 0))
    for g in range(groups):
        sl = slice(g * gd, (g + 1) * gd)
        w = jnp.where(tril, ws_ref[g, 0:ct, :], 0.0).astype(BF16)
        f = _dot(w, vn_b[:, sl]) + bst_ref[0:ct, g:g + 1]
        sg = _gelu(zu_ref[:, sl]) * f
        merged = _sigmoid(ga_ref[:, sl]) * hm_ref[:, sl] + _sigmoid(gb_ref[:, sl]) * sg
        mg_ref[:, sl] = merged.astype(mg_ref.dtype)


def _gmlp(z3, cols, hm, lng3, lnb3, w_s, b_st, layer, ct, cs, emit_vn, merged_dtype):
    b, l, _ = z3.shape
    gw = hm.shape[-1]
    groups = w_s.shape[1]
    gd = gw // groups
    nc = l // ct

    def zspec(name):
        assert cols[name] % gw == 0
        return pl.BlockSpec((None, ct, gw), lambda i, c: (i, c, cols[name] // gw))

    in_specs = [zspec("zu"), zspec("zv"), zspec("ga"), zspec("gb"),
                pl.BlockSpec((None, ct, gw), lambda i, c: (i, c, 0)),
                pl.BlockSpec((None, 1, gw), lambda i, c: (layer, 0, 0)),
                pl.BlockSpec((None, 1, gw), lambda i, c: (layer, 0, 0)),
                pl.BlockSpec((None, groups, cs, cs), lambda i, c: (layer, 0, 0, 0)),
                pl.BlockSpec((None, cs, groups), lambda i, c: (layer, 0, 0))]
    out_specs = [pl.BlockSpec((None, ct, gw), lambda i, c: (i, c, 0))]
    out_shape = [jax.ShapeDtypeStruct((b, l, gw), merged_dtype)]
    if emit_vn:
        out_specs.append(pl.BlockSpec((None, ct, gw), lambda i, c: (i, c, 0)))
        out_shape.append(jax.ShapeDtypeStruct((b, l, gw), F32))
    return pl.pallas_call(
        functools.partial(_gmlp_body, ct, cs, groups, gd, emit_vn),
        grid=(b, nc),
        in_specs=in_specs,
        out_specs=out_specs,
        out_shape=out_shape,
        compiler_params=_cparams(2),
        name="gmlp_merge",
    )(z3, z3, z3, z3, hm, lng3, lnb3, w_s, b_st)


def _outproj_body(a_ref, w_ref, x_ref, gate_ref, o_ref):
    acc = _dot(a_ref[...].astype(BF16), w_ref[...])
    o_ref[...] = x_ref[...] + gate_ref[...] * acc.reshape(x_ref.shape)


def _outproj(merged2, w, layer, x3, mod4, k_gate, tm, tn):
    b, l, d = x3.shape
    bb, r, nt, ridx = _row_tiling(b, l, tm)
    kdim = merged2.shape[1]
    return pl.pallas_call(
        _outproj_body,
        grid=(nt, d // tn),
        in_specs=[pl.BlockSpec((tm, kdim), lambda i, j: (i, 0)),
                  pl.BlockSpec((None, kdim, tn), lambda i, j: (layer, 0, j)),
                  pl.BlockSpec((bb, r, tn), lambda i, j: ridx(i) + (j,)),
                  pl.BlockSpec((bb, None, 1, tn), lambda i, j: (ridx(i)[0], k_gate, 0, j))],
        out_specs=pl.BlockSpec((bb, r, tn), lambda i, j: ridx(i) + (j,)),
        out_shape=jax.ShapeDtypeStruct((b, l, d), F32),
        compiler_params=_cparams(2),
        name="outproj_residual",
    )(merged2, w, x3, mod4)


def _peer_pairs():
    k = P_TOPK + 1
    return [(a, b) for a in range(k) for b in range(k) if (a + 1) * (b + 1) <= k]


def _topk_body(nk, tt, pq_ref, keys_ref, st_ref, aux_ref, topv, topi, cand, cid):
    p = pl.program_id(1) % 2
    riota = lax.broadcasted_iota(jnp.int32, (nk, tt), 0).astype(F32)
    s = _dot_nt(keys_ref[...].astype(BF16), pq_ref[...].astype(BF16))
    st_ref[...] = s
    x = s
    for a in range(P_TOPK + 1):
        m = jnp.max(x, axis=0, keepdims=True)
        idx = jnp.min(jnp.where(x == m, riota, float(nk)), axis=0, keepdims=True)
        topv[p, a:a + 1, :] = m
        topi[p, a:a + 1, :] = idx
        x = jnp.where(riota == idx, -jnp.inf, x)

    @pl.when(p == 1)
    def _():
        pairs = _peer_pairs()
        npad = cand.shape[0]
        cand[len(pairs):npad, :] = jnp.full((npad - len(pairs), tt), -jnp.inf, F32)
        cid[len(pairs):npad, :] = jnp.full((npad - len(pairs), tt), 0.0, F32)
        for r, (a, b) in enumerate(pairs):
            cand[r:r + 1, :] = topv[0, a:a + 1, :] + topv[1, b:b + 1, :]
            cid[r:r + 1, :] = topi[0, a:a + 1, :] * float(nk) + topi[1, b:b + 1, :]
        x = cand[...]
        ids = cid[...]
        big = float(nk * nk)
        mx = None
        z = None
        for a in range(P_TOPK):
            m = jnp.max(x, axis=0, keepdims=True)
            e = jnp.min(jnp.where(x == m, ids, big), axis=0, keepdims=True)
            if a == 0:
                mx = m
                z = jnp.ones_like(m)
            else:
                z = z + jnp.exp(m - mx)
            x = jnp.where(ids == e, -jnp.inf, x)
        runner_up = jnp.max(x, axis=0, keepdims=True)
        aux_ref[0:1, :] = m
        aux_ref[1:2, :] = e
        aux_ref[2:3, :] = mx + jnp.log(z)
        aux_ref[3:4, :] = jnp.where(runner_up == m, 1.0, 0.0)
        aux_ref[4:8, :] = jnp.zeros((4, tt), F32)


def _peer_topk(pq, keys4, layer, tt):
    n, qw = pq.shape
    hp, nk, half = keys4.shape[1:]
    heads = hp // 2
    npairs = -(-len(_peer_pairs()) // SUBLANES) * SUBLANES
    ntop = -(-(P_TOPK + 1) // SUBLANES) * SUBLANES
    return tuple(pl.pallas_call(
        functools.partial(_topk_body, nk, tt),
        grid=(n // tt, hp),
        in_specs=[pl.BlockSpec((tt, half), lambda i, k: (i, k)),
                  pl.BlockSpec((None, None, nk, half), lambda i, k: (layer, k, 0, 0))],
        out_specs=[pl.BlockSpec((None, nk, tt), lambda i, k: (k, 0, i)),
                   pl.BlockSpec((None, SUBLANES, tt), lambda i, k: (k // 2, 0, i))],
        out_shape=[jax.ShapeDtypeStruct((hp, nk, n), F32),
                   jax.ShapeDtypeStruct((heads, SUBLANES, n), F32)],
        scratch_shapes=[pltpu.VMEM((2, ntop, tt), F32), pltpu.VMEM((2, ntop, tt), F32),
                        pltpu.VMEM((npairs, tt), F32), pltpu.VMEM((npairs, tt), F32)],
        compiler_params=_cparams(2),
        name="peer_topk",
    )(pq, keys4))


def _peer_body(heads, nk, tt, ec, tie_ref, h2_ref, st_ref, aux_ref, u_ref, v_ref, o_ref, at_scr, w_scr, p_scr):
    j = pl.program_id(1)
    rt = 4 * SUBLANES
    rows = lax.broadcasted_iota(jnp.int32, (rt, LANES), 0)
    per = ec // nk
    grp = SUBLANES // per
    base = pl.multiple_of((j // grp) * SUBLANES, SUBLANES)
    sub = j % grp

    def weights(exact_ties):
        for ii in range(per):
            i_glob = j * per + ii
            for lb in range(tt // LANES):
                ln = slice(lb * LANES, (lb + 1) * LANES)
                for r in range(nk // rt):
                    wacc = jnp.zeros((rt, LANES), F32)
                    for h in range(heads):
                        tile = st_ref[2 * h, pl.ds(base, SUBLANES), ln]
                        s0b = tile[ii:ii + 1]
                        for k in range(1, grp):
                            s0b = jnp.where(sub == k, tile[k * per + ii:k * per + ii + 1], s0b)
                        cnd = st_ref[2 * h + 1, r * rt:(r + 1) * rt, ln] + s0b
                        thr = aux_ref[h, 0:1, ln]
                        c0 = aux_ref[h, 2:3, ln]
                        if exact_ties:
                            eio = (i_glob * nk + r * rt + rows).astype(F32)
                            sel = (cnd > thr) | ((cnd == thr) & (eio <= aux_ref[h, 1:2, ln]))
                        else:
                            sel = cnd >= thr
                        wacc = wacc + jnp.where(sel, jnp.exp(cnd - c0), 0.0)
                    w_scr[ii * nk + r * rt:ii * nk + (r + 1) * rt, ln] = wacc

    tie = tie_ref[pl.program_id(0)]

    @pl.when(tie == 0)
    def _():
        weights(False)

    @pl.when(tie != 0)
    def _():
        weights(True)

    at_scr[...] = _dot_nt(u_ref[...], h2_ref[...])
    for ii in range(per):
        for lb in range(tt // LANES):
            ln = slice(lb * LANES, (lb + 1) * LANES)
            pt = w_scr[ii * nk:(ii + 1) * nk, ln] * _gelu(at_scr[ii * nk:(ii + 1) * nk, ln])
            p_scr[ln, ii * nk:(ii + 1) * nk] = pt.T.astype(BF16)

    @pl.when(j == 0)
    def _():
        o_ref[...] = jnp.zeros(o_ref.shape, F32)

    d = o_ref.shape[1]
    cw = min(d, 1024)
    for cb in range(d // cw):
        o_ref[:, cb * cw:(cb + 1) * cw] += _dot(p_scr[...], v_ref[:, cb * cw:(cb + 1) * cw])


def _peer_dense(h2, st, aux, tie, u_tab, v_tab, layer, tt, ec):
    n, d = h2.shape
    e = u_tab.shape[1]
    hp, nk, _ = st.shape
    heads = hp // 2
    assert nk == LANES and e == nk * nk and ec % nk == 0 and tt % LANES == 0
    assert SUBLANES % (ec // nk) == 0
    return pl.pallas_call(
        functools.partial(_peer_body, heads, nk, tt, ec),
        grid_spec=pltpu.PrefetchScalarGridSpec(
            num_scalar_prefetch=1,
            grid=(n // tt, e // ec),
            in_specs=[pl.BlockSpec((tt, d), lambda i, j, f: (i, 0)),
                      pl.BlockSpec((hp, nk, tt), lambda i, j, f: (0, 0, i)),
                      pl.BlockSpec((heads, SUBLANES, tt), lambda i, j, f: (0, 0, i)),
                      pl.BlockSpec((None, ec, d), lambda i, j, f: (layer, j, 0)),
                      pl.BlockSpec((None, ec, d), lambda i, j, f: (layer, j, 0))],
            out_specs=pl.BlockSpec((tt, d), lambda i, j, f: (i, 0)),
            scratch_shapes=[pltpu.VMEM((ec, tt), F32), pltpu.VMEM((ec, tt), F32), pltpu.VMEM((tt, ec), BF16)]),
        out_shape=jax.ShapeDtypeStruct((n, d), F32),
        compiler_params=_cparams(2),
        name="peer_dense",
    )(tie, h2, st, aux, u_tab, v_tab)


def _resid_body(final, x_ref, p_ref, gate_ref, *rest):
    x = x_ref[...] + gate_ref[...] * p_ref[...]
    if final:
        gf_ref, o_ref = rest
        o_ref[...] = x * lax.rsqrt(jnp.mean(x * x, axis=-1, keepdims=True) + EPS) * gf_ref[...]
    else:
        rest[0][...] = x


def _residual(x3, p3, mod4, k_gate, tm, g_final2=None):
    b, l, d = x3.shape
    bb, r, nt, ridx = _row_tiling(b, l, tm)
    final = g_final2 is not None
    in_specs = [pl.BlockSpec((bb, r, d), lambda i: ridx(i) + (0,)),
                pl.BlockSpec((bb, r, d), lambda i: ridx(i) + (0,)),
                pl.BlockSpec((bb, None, 1, d), lambda i: (ridx(i)[0], k_gate, 0, 0))]
    args = [x3, p3, mod4]
    if final:
        in_specs.append(pl.BlockSpec((1, d), lambda i: (0, 0)))
        args.append(g_final2)
    return pl.pallas_call(
        functools.partial(_resid_body, final),
        grid=(nt,),
        in_specs=in_specs,
        out_specs=pl.BlockSpec((bb, r, d), lambda i: ridx(i) + (0,)),
        out_shape=jax.ShapeDtypeStruct((b, l, d), F32),
        compiler_params=_cparams(1),
        name="residual",
    )(*args)


def _pick(n, pref):
    t = min(n, pref)
    while n % t:
        t //= 2
    return t


def _trunk_layer(x3, mod4, l_real, conv_hist, c0, n0, m0, state_layer, lw, layer, g_final2, want_vn,
                 prev_c=None):
    b, l, d = x3.shape
    n = b * l
    heads, dk = lw["wq"].shape[1], lw["wq"].shape[2]
    qkw = heads * dk
    cols = lw["cols"]
    chunk = LANES
    ct = min(chunk, l)
    tm = _pick(n, 512)

    z, gates = _modnorm_matmul(x3, lw["g_mix"], mod4, 1, 0, lw["w_in"], layer, tm,
                               _pick(lw["w_in"].shape[2], 1024), w_small=lw["w_gate"])
    z3 = z.reshape(b, l, -1)
    gt = jnp.swapaxes(gates[:, :2 * heads].reshape(b, l, 2 * heads), 1, 2)
    if l % chunk:
        gt = jnp.pad(gt, ((0, 0), (0, 0), (0, chunk - l % chunk)))
    conv_new = z3[:, l_real - (CONV_W - 1):l_real, cols["qk"]:cols["qk"] + qkw]
    hm, c1, n1, m1 = _mlstm(z3, cols, gt, conv_hist, lw["conv_w"], lw["conv_b"], lw["wq"], lw["wk"],
                            lw["b_i"], lw["b_f"], lw["mnorm_g"], c0, n0, m0,
                            layer, state_layer, ct, chunk, l_real, prev_c)
    outs = _gmlp(z3, cols, hm, lw["ln_g"], lw["ln_b"], lw["w_s"], lw["b_st"], layer, ct, chunk,
                 want_vn, BF16 if ct % 16 == 0 else F32)
    merged = outs[0]
    vn = outs[1] if want_vn else None
    x3 = _outproj(merged.reshape(n, -1), lw["w_out"], layer, x3, mod4, 2, tm, _pick(d, 1024))

    pq, h2 = _modnorm_matmul(x3, lw["g_ffn"], mod4, 4, 3, lw["w_pq"], layer, tm,
                             _pick(lw["w_pq"].shape[2], 512), emit_h=True)
    if l_real < l:
        pq = pq.reshape(b, l, -1)[:, :l_real].reshape(b * l_real, -1)
        h2 = h2.reshape(b, l, -1)[:, :l_real].reshape(b * l_real, -1)
    nr = b * l_real
    tt = _pick(nr, 512)
    tk = _pick(nr, 1024)
    st, aux = _peer_topk(pq, lw["keys"], layer, tk)
    tie = (jnp.max(aux[:, 3, :].reshape(-1, nr // tt, tt), axis=(0, 2)) > 0).astype(jnp.int32)
    peer = _peer_dense(h2, st, aux, tie, lw["u_tab"], lw["v_tab"], layer, tt,
                       _pick(lw["u_tab"].shape[1], 512))
    peer = peer.reshape(b, l_real, d)
    if l_real < l:
        peer = jnp.pad(peer, ((0, 0), (0, l - l_real), (0, 0)))
    x3 = _residual(x3, peer, mod4, 5, _pick(n, 256), g_final2)
    return x3, conv_new, c1, n1, m1[:, :, 0], vn


def kernel(x_prompt, x_sample, state_C, state_n, state_m, state_conv, c_prompt, c_sample, w_ada, b_ada, g_mix, w_in, conv_w, conv_b, wq_m, wk_m, b_i, b_f, mnorm_g, ln_g, ln_b, w_s, b_s, w_out, g_ffn, w_pq, sub_keys, u_tab, v_tab, g_final):
    depth = w_ada.shape[0]
    bp, lp, d = x_prompt.shape
    bs, ls, _ = x_sample.shape
    heads, dk = wq_m.shape[1], wq_m.shape[2]
    dv = state_C.shape[-1]
    qkw, vw = heads * dk, heads * dv
    gw = ln_g.shape[1]

    ls_pad = -(-ls // SUBLANES) * SUBLANES
    xs = jnp.pad(x_sample, ((0, 0), (0, ls_pad - ls), (0, 0)))
    xp = x_prompt

    n_c = bp + bs
    n_c_pad = -(-n_c // SUBLANES) * SUBLANES
    c_all = jnp.pad(jnp.concatenate([c_prompt, c_sample], axis=0), ((0, n_c_pad - n_c), (0, 0)))

    o_v, o_o, o_g = qkw, qkw + vw, qkw + 2 * vw
    o_zu = o_g + 2 * heads
    cols = {"v": 0, "o": vw, "zu": 2 * vw, "zv": 2 * vw + gw, "ga": 2 * vw + 2 * gw,
            "gb": 2 * vw + 2 * gw + d, "qk": 2 * vw + 2 * gw + 2 * d}

    hist_pad = SUBLANES - (CONV_W - 1)
    zero_hist = jnp.zeros((bp, SUBLANES, qkw), F32)
    zero_c = jnp.zeros((1, bp, heads, dk, dv), F32)
    zero_n = jnp.zeros((1, bp, heads, dk), F32)
    zero_m = jnp.zeros((1, bp, heads, 1), F32)
    state_m4 = state_m[..., None]
    g_final2 = g_final.reshape(1, d)
    b_ada3 = b_ada[:, None, :]

    p_out = [[], [], [], []]
    s_out = [[], [], [], [], []]
    lw = {
        "cols": cols,
        "g_mix": g_mix[:, None, :], "g_ffn": g_ffn[:, None, :],
        "w_in": _regroup_w_in(jnp.swapaxes(w_in, 1, 2), qkw, vw, 2 * heads, _pick(qkw, 1024), _pick(d, 512)),
        "w_gate": jnp.pad(w_in[:, :, o_g:o_zu], ((0, 0), (0, 0), (0, LANES - 2 * heads))).astype(BF16),
        "conv_w": conv_w, "conv_b": conv_b[:, None, :],
        "wq": wq_m.astype(BF16), "wk": wk_m.astype(BF16),
        "b_i": b_i[:, :, None], "b_f": b_f[:, :, None],
        "mnorm_g": mnorm_g[:, None, :], "ln_g": ln_g[:, None, :], "ln_b": ln_b[:, None, :],
        "w_s": w_s, "b_st": jnp.swapaxes(b_s, 1, 2),
        "w_out": w_out.astype(BF16), "w_pq": w_pq.astype(BF16),
        "keys": sub_keys.reshape(depth, -1, sub_keys.shape[3], sub_keys.shape[4]),
        "u_tab": u_tab.astype(BF16), "v_tab": v_tab.astype(BF16),
    }
    for l in range(depth):
        mod = _adaln(c_all, w_ada, b_ada3, l, _pick(w_ada.shape[2], 512))
        mod_p = mod[:bp].reshape(bp, N_MOD, 1, d)
        mod_s = mod[bp:n_c].reshape(bs, N_MOD, 1, d)
        last = l == depth - 1

        xp, cv, c1, n1, m1, _ = _trunk_layer(xp, mod_p, lp, zero_hist, zero_c, zero_n, zero_m, 0,
                                             lw, l, g_final2 if last else None, False)
        for acc, val in zip(p_out, (c1, n1, m1, cv)):
            acc.append(val)

        hist_s = jnp.pad(state_conv[l], ((0, 0), (hist_pad, 0), (0, 0)))
        stack_c = last and depth == 2
        xs, cv, c1, n1, m1, vn = _trunk_layer(xs, mod_s, ls, hist_s, state_C, state_n, state_m4, l,
                                              lw, l, g_final2 if last else None, True,
                                              s_out[0][0] if stack_c else None)
        if stack_c:
            sample_c = c1
        for acc, val in zip(s_out, (c1, n1, m1, cv, vn[:, :ls])):
            acc.append(val)

    s_stacked = [jnp.stack(a) for a in s_out[1:]]
    s_stacked.insert(0, sample_c if depth == 2 else jnp.stack(s_out[0]))
    return (xp, xs[:, :ls]) + tuple(jnp.stack(a) for a in p_out) + tuple(s_stacked)
```

```python
import functools

import jax
import jax.numpy as jnp
from jax import lax
from jax.experimental import pallas as pl
from jax.experimental.pallas import tpu as pltpu

F32 = jnp.float32
BF16 = jnp.bfloat16
EPS = 1e-6
NEG_BIG = -1e30
LANES = 128
SUBLANES = 8
VMEM_LIMIT = 56 * 1024 * 1024
CONV_W = 4
P_TOPK = 16
N_MOD = 6
SQRT_HALF = 0.7071067811865476


def _cparams(n_axes):
    return pltpu.CompilerParams(dimension_semantics=("arbitrary",) * n_axes,
                                vmem_limit_bytes=VMEM_LIMIT)


def _gelu(x):
    return 0.5 * x * (1.0 + lax.erf(x * SQRT_HALF))


def _sigmoid(x):
    return 1.0 / (1.0 + jnp.exp(-x))


def _log_sigmoid(x):
    return jnp.minimum(x, 0.0) - jnp.log(1.0 + jnp.exp(-jnp.abs(x)))


def _dot(a, b):
    return jnp.dot(a, b, preferred_element_type=F32)


def _dot_nt(a, b):
    return lax.dot_general(a, b, (((1,), (1,)), ((), ())), preferred_element_type=F32)


def _dot_tn(a, b):
    return lax.dot_general(a, b, (((0,), (0,)), ((), ())), preferred_element_type=F32)


def _split3(a):
    a1 = a.astype(BF16)
    r1 = a - a1.astype(F32)
    a2 = r1.astype(BF16)
    a3 = (r1 - a2.astype(F32)).astype(BF16)
    return a1, a2, a3


def _adaln_body(c_ref, w_ref, b_ref, o_ref):
    c = c_ref[...]
    a = c * _sigmoid(c)
    a1 = a.astype(BF16)
    a2 = (a - a1.astype(F32)).astype(BF16)
    w1 = w_ref[...].astype(BF16)
    o_ref[...] = _dot(a1, w1) + _dot(a2, w1) + b_ref[...]


def _adaln(c_all, w_ada, b_ada3, layer, tn):
    m, d = c_all.shape
    n = w_ada.shape[2]
    return pl.pallas_call(
        _adaln_body,
        grid=(n // tn,),
        in_specs=[pl.BlockSpec((m, d), lambda j: (0, 0)),
                  pl.BlockSpec((None, d, tn), lambda j: (layer, 0, j)),
                  pl.BlockSpec((None, 1, tn), lambda j: (layer, 0, j))],
        out_specs=pl.BlockSpec((m, tn), lambda j: (0, j)),
        out_shape=jax.ShapeDtypeStruct((m, n), F32),
        compiler_params=_cparams(1),
        name="adaln",
    )(c_all, w_ada, b_ada3)


def _regroup_body(a_ref, o_ref):
    o_ref[...] = a_ref[...].T.astype(BF16)


def _regroup_w_in(w_in_t, qkw, vw, n_gate, tn, tr):
    depth, in_w, d = w_in_t.shape
    o_v, o_g = qkw, qkw + 2 * vw
    o_zu = o_g + n_gate
    assert (o_g - o_v) % tn == 0 and (in_w - o_zu) % tn == 0 and qkw % tn == 0 and o_zu % SUBLANES == 0
    n_a, n_b, n_c = (o_g - o_v) // tn, (in_w - o_zu) // tn, qkw // tn

    def src_map(l, r, j):
        row = jnp.where(j < n_a, o_v + j * tn,
                        jnp.where(j < n_a + n_b, o_zu + (j - n_a) * tn, (j - n_a - n_b) * tn))
        return (l, pl.multiple_of(row, SUBLANES), r * tr)

    return pl.pallas_call(
        _regroup_body,
        grid=(depth, d // tr, n_a + n_b + n_c),
        in_specs=[pl.BlockSpec((None, pl.Element(tn), pl.Element(tr)), src_map)],
        out_specs=pl.BlockSpec((None, tr, tn), lambda l, r, j: (l, r, j)),
        out_shape=jax.ShapeDtypeStruct((depth, d, (n_a + n_b + n_c) * tn), BF16),
        compiler_params=_cparams(3),
        name="regroup_w_in",
    )(w_in_t)


def _row_tiling(b, l, tm):
    if l >= tm:
        assert l % tm == 0
        per = l // tm
        return 1, tm, b * per, (lambda i: (i // per, i % per))
    assert tm % l == 0 and b % (tm // l) == 0
    bb = tm // l
    return bb, l, b // bb, (lambda i: (i, 0))


def _modnorm_body(has_small, emit_h, tm, x_ref, g_ref, sc_ref, sh_ref, w_ref, *rest):
    rest = list(rest)
    ws_ref = rest.pop(0) if has_small else None
    o_ref = rest.pop(0)
    os_ref = rest.pop(0) if has_small else None
    ho_ref = rest.pop(0) if emit_h else None
    h_scr = rest.pop(0)

    @pl.when(pl.program_id(1) == 0)
    def _():
        bb, r, d = x_ref.shape
        nsub = 4
        while bb > 1 and bb % nsub:
            nsub //= 2
        rows = tm // nsub

        def sub(k, carry):
            if bb == 1:
                sl = (slice(None), pl.ds(pl.multiple_of(k * rows, rows), rows), slice(None))
                sc, sh = sc_ref[...], sh_ref[...]
            else:
                bsub = pl.ds(pl.multiple_of(k * (bb // nsub), bb // nsub), bb // nsub)
                sl = (bsub, slice(None), slice(None))
                sc, sh = sc_ref[bsub], sh_ref[bsub]
            x = x_ref[sl]
            var = jnp.mean(x * x, axis=-1, keepdims=True)
            y = x * lax.rsqrt(var + EPS) * g_ref[...]
            h = y * (1.0 + sc) + sh
            h_scr[pl.ds(pl.multiple_of(k * rows, rows), rows), :] = h.reshape(rows, d).astype(BF16)
            return carry

        lax.fori_loop(0, nsub, sub, 0)
        if has_small:
            os_ref[...] = _dot(h_scr[...], ws_ref[...])
        if emit_h:
            ho_ref[...] = h_scr[...]

    o_ref[...] = _dot(h_scr[...], w_ref[...])


def _modnorm_matmul(x3, g3, mod4, k_scale, k_shift, w, layer, tm, tn, w_small=None, emit_h=False):
    b, l, d = x3.shape
    nout = w.shape[2]
    bb, r, nt, ridx = _row_tiling(b, l, tm)
    has_small = w_small is not None

    def xmap(i, j):
        bi, ri = ridx(i)
        return (bi, ri, 0)

    def modmap(k):
        def f(i, j):
            return (ridx(i)[0], k, 0, 0)
        return f

    in_specs = [pl.BlockSpec((bb, r, d), xmap),
                pl.BlockSpec((None, 1, d), lambda i, j: (layer, 0, 0)),
                pl.BlockSpec((bb, None, 1, d), modmap(k_scale)),
                pl.BlockSpec((bb, None, 1, d), modmap(k_shift)),
                pl.BlockSpec((None, d, tn), lambda i, j: (layer, 0, j))]
    args = [x3, g3, mod4, mod4, w]
    out_specs = [pl.BlockSpec((tm, tn), lambda i, j: (i, j))]
    out_shape = [jax.ShapeDtypeStruct((b * l, nout), F32)]
    if has_small:
        ns = w_small.shape[2]
        in_specs.append(pl.BlockSpec((None, d, ns), lambda i, j: (layer, 0, 0)))
        args.append(w_small)
        out_specs.append(pl.BlockSpec((tm, ns), lambda i, j: (i, 0)))
        out_shape.append(jax.ShapeDtypeStruct((b * l, ns), F32))
    if emit_h:
        out_specs.append(pl.BlockSpec((tm, d), lambda i, j: (i, 0)))
        out_shape.append(jax.ShapeDtypeStruct((b * l, d), BF16))
    return pl.pallas_call(
        functools.partial(_modnorm_body, has_small, emit_h, tm),
        grid=(nt, nout // tn),
        in_specs=in_specs,
        out_specs=out_specs,
        out_shape=out_shape,
        scratch_shapes=[pltpu.VMEM((tm, d), BF16)],
        compiler_params=_cparams(2),
        name="modnorm_matmul",
    )(*args)


def _mlstm_body(ct, cs, l_real, heads, dk, dv, stacked,
                qk_ref, v_ref, o_ref, gt_ref, cb_ref, cw_ref, cbias_ref, wq_ref, wk_ref,
                bi_ref, bf_ref, mg_ref, c0_ref, n0_ref, m0_ref, *rest):
    if stacked:
        prev_ref, hm_ref, call_ref, n_ref, m_ref, xbuf = rest
        c_ref = call_ref.at[1]
    else:
        hm_ref, c_ref, n_ref, m_ref, xbuf = rest
    hist = SUBLANES

    @pl.when(pl.program_id(1) == 0)
    def _():
        if stacked:
            call_ref[0] = prev_ref[...]
        c_ref[...] = c0_ref[...]
        n_ref[...] = n0_ref[...]
        m_ref[...] = jnp.broadcast_to(m0_ref[...], m_ref.shape)
        xbuf[0:hist, :] = cb_ref[...]

    xbuf[hist:hist + ct, :] = qk_ref[...]
    acc = jnp.broadcast_to(cbias_ref[...], (ct, heads * dk))
    for j in range(CONV_W):
        acc = acc + xbuf[pl.ds(hist - (CONV_W - 1) + j, ct), :] * cw_ref[j:j + 1, :]
    xc = (acc * _sigmoid(acc)).astype(BF16)
    xbuf[0:hist, :] = xbuf[ct:ct + hist, :]

    g = gt_ref[...]
    ig_all = g[0:heads] + bi_ref[...]
    lf_all = _log_sigmoid(g[heads:2 * heads] + bf_ref[...])
    if l_real < cs:
        valid = lax.broadcasted_iota(jnp.int32, (heads, cs), 1) < l_real
        ig_all = jnp.where(valid, ig_all, NEG_BIG)
        lf_all = jnp.where(valid, lf_all, 0.0)
    upper = (lax.broadcasted_iota(jnp.int32, (cs, cs), 0)
             <= lax.broadcasted_iota(jnp.int32, (cs, cs), 1)).astype(BF16)
    l1, l2, l3 = _split3(lf_all)
    f_all = _dot(l1, upper) + _dot(l2, upper) + _dot(l3, upper)

    tt = lax.broadcasted_iota(jnp.int32, (ct, cs), 0)
    ss = lax.broadcasted_iota(jnp.int32, (ct, cs), 1)
    diag = tt == ss
    tril = ss <= tt
    pad_rows = cs - ct

    m_all = m_ref[...]
    n_all = n_ref[...]

    def gates_and_projections(h):
        ig_r = ig_all[h:h + 1, :]
        f_r = f_all[h:h + 1, :]
        m0 = m_all[h:h + 1, 0:1]
        f_c = jnp.sum(jnp.where(diag, f_r, 0.0), axis=1, keepdims=True)
        ig_c = jnp.sum(jnp.where(diag, ig_r, 0.0), axis=1, keepdims=True)
        cm_c = jnp.max(jnp.where(tril, ig_r - f_r, -jnp.inf), axis=1, keepdims=True)
        m_c = f_c + jnp.maximum(m0, cm_c)
        dmat = jnp.exp(jnp.where(tril, f_c - f_r + ig_r - m_c, -jnp.inf))
        a_c = jnp.exp(f_c + m0 - m_c)
        xh = xc[:, h * dk:(h + 1) * dk]
        q = _dot(xh, wq_ref[h])
        k = _dot(xh, wk_ref[h]) * (dk ** -0.5)
        v = v_ref[:, h * dv:(h + 1) * dv]
        m_l = m_c[ct - 1:ct, :]
        f_l = f_c[ct - 1:ct, :]
        ws_c = jnp.exp(f_l - f_c + ig_c - m_l)
        a_l = jnp.exp(f_l + m0 - m_l)
        kw = k * ws_c
        if pad_rows:
            k_p = jnp.concatenate([k, jnp.zeros((pad_rows, dk), F32)], axis=0)
            kw_p = jnp.concatenate([kw, jnp.zeros((pad_rows, dk), F32)], axis=0)
            v_p = jnp.concatenate([v, jnp.zeros((pad_rows, dv), F32)], axis=0)
        else:
            k_p, kw_p, v_p = k, kw, v
        return dict(q=q, q_b=q.astype(BF16), k_b=k_p.astype(BF16), kw=kw, kw_b=kw_p.astype(BF16),
                    v_b=v_p.astype(BF16), dmat=dmat, a_c=a_c, a_l=a_l, m_c=m_c, m_l=m_l)

    def first_matmuls(h, t):
        c_old = c_ref[h]
        t["s"] = _dot_nt(t["q_b"], t["k_b"]) * t["dmat"]
        t["qc"] = _dot(t["q_b"], c_old.astype(BF16))
        return t

    def update_memory(h, t):
        c_ref[h] = t["a_l"] * c_ref[h] + _dot_tn(t["kw_b"], t["v_b"])

    def outputs(h, t):
        n_old = n_all[h:h + 1, :]
        s, a_c = t["s"], t["a_c"]
        num = a_c * t["qc"] + _dot(s.astype(BF16), t["v_b"])
        dotn = (a_c * jnp.sum(t["q"] * n_old, axis=1, keepdims=True)
                + jnp.sum(s, axis=1, keepdims=True))
        den = jnp.maximum(jnp.abs(dotn), jnp.exp(-t["m_c"])) + EPS
        hh = num / den
        hn = hh * lax.rsqrt(jnp.mean(hh * hh, axis=1, keepdims=True) + EPS)
        hn = hn * mg_ref[:, h * dv:(h + 1) * dv]
        hm_ref[:, h * dv:(h + 1) * dv] = hn * _sigmoid(o_ref[:, h * dv:(h + 1) * dv])
        n_new = t["a_l"] * n_old + jnp.sum(t["kw"], axis=0, keepdims=True)
        return n_new, jnp.broadcast_to(t["m_l"], (1, m_ref.shape[1]))

    if pad_rows:
        ts = [gates_and_projections(h) for h in range(heads)]
        ts = [first_matmuls(h, ts[h]) for h in range(heads)]
        for h in range(heads):
            update_memory(h, ts[h])
        new = [outputs(h, ts[h]) for h in range(heads)]
    else:
        new = []
        for h in range(heads):
            t = first_matmuls(h, gates_and_projections(h))
            new.append(outputs(h, t))
            update_memory(h, t)

    for h in range(heads):
        n_ref[h:h + 1, :] = new[h][0]
        m_ref[h:h + 1, :] = new[h][1]


def _mlstm(z3, cols, gt, conv_hist, conv_w, conv_b3, wq, wk, bi3, bf3, mg3, c0, n0, m0,
           layer, state_layer, ct, cs, l_real, prev_c=None):
    b, l, _ = z3.shape
    heads, dk = wq.shape[1], wq.shape[2]
    dv = c0.shape[-1]
    qkw, vw = heads * dk, heads * dv
    nc = l // ct
    stacked = prev_c is not None
    assert cols["qk"] % qkw == 0 and cols["v"] % vw == 0 and cols["o"] % vw == 0
    body = functools.partial(_mlstm_body, ct, cs, l_real, heads, dk, dv, stacked)
    in_specs = [
        pl.BlockSpec((None, ct, qkw), lambda i, c: (i, c, cols["qk"] // qkw)),
        pl.BlockSpec((None, ct, vw), lambda i, c: (i, c, cols["v"] // vw)),
        pl.BlockSpec((None, ct, vw), lambda i, c: (i, c, cols["o"] // vw)),
        pl.BlockSpec((None, 2 * heads, cs), lambda i, c: (i, 0, c)),
        pl.BlockSpec((None, SUBLANES, qkw), lambda i, c: (i, 0, 0)),
        pl.BlockSpec((None, CONV_W, qkw), lambda i, c: (layer, 0, 0)),
        pl.BlockSpec((None, 1, qkw), lambda i, c: (layer, 0, 0)),
        pl.BlockSpec((None, heads, dk, dk), lambda i, c: (layer, 0, 0, 0)),
        pl.BlockSpec((None, heads, dk, dk), lambda i, c: (layer, 0, 0, 0)),
        pl.BlockSpec((None, heads, 1), lambda i, c: (layer, 0, 0)),
        pl.BlockSpec((None, heads, 1), lambda i, c: (layer, 0, 0)),
        pl.BlockSpec((None, 1, vw), lambda i, c: (layer, 0, 0)),
        pl.BlockSpec((None, None, heads, dk, dv), lambda i, c: (state_layer, i, 0, 0, 0)),
        pl.BlockSpec((None, None, heads, dk), lambda i, c: (state_layer, i, 0, 0)),
        pl.BlockSpec((None, None, heads, 1), lambda i, c: (state_layer, i, 0, 0)),
    ]
    out_specs = [
        pl.BlockSpec((None, ct, vw), lambda i, c: (i, c, 0)),
        pl.BlockSpec((None, heads, dk, dv), lambda i, c: (i, 0, 0, 0)),
        pl.BlockSpec((None, heads, dk), lambda i, c: (i, 0, 0)),
        pl.BlockSpec((None, heads, LANES), lambda i, c: (i, 0, 0)),
    ]
    out_shape = [
        jax.ShapeDtypeStruct((b, l, vw), F32),
        jax.ShapeDtypeStruct((b, heads, dk, dv), F32),
        jax.ShapeDtypeStruct((b, heads, dk), F32),
        jax.ShapeDtypeStruct((b, heads, LANES), F32),
    ]
    args = [z3, z3, z3, gt, conv_hist, conv_w, conv_b3, wq, wk, bi3, bf3, mg3, c0, n0, m0]
    if stacked:
        in_specs.append(pl.BlockSpec((None, heads, dk, dv), lambda i, c: (i, 0, 0, 0)))
        args.append(prev_c)
        out_specs[1] = pl.BlockSpec((2, None, heads, dk, dv), lambda i, c: (0, i, 0, 0, 0))
        out_shape[1] = jax.ShapeDtypeStruct((2, b, heads, dk, dv), F32)
    return pl.pallas_call(
        body,
        grid=(b, nc),
        in_specs=in_specs,
        out_specs=out_specs,
        out_shape=out_shape,
        scratch_shapes=[pltpu.VMEM((ct + SUBLANES, qkw), F32)],
        compiler_params=_cparams(2),
        name="mlstm",
    )(*args)


def _gmlp_body(ct, cs, groups, gd, emit_vn, zu_ref, zv_ref, ga_ref, gb_ref, hm_ref, lng_ref, lnb_ref,
               ws_ref, bst_ref, *outs):
    mg_ref = outs[0]
    gv = _gelu(zv_ref[...])
    mu = jnp.mean(gv, axis=-1, keepdims=True)
    xc = gv - mu
    var = jnp.mean(xc * xc, axis=-1, keepdims=True)
    vn = xc * lax.rsqrt(var + EPS) * lng_ref[...] + lnb_ref[...]
    if emit_vn:
        outs[1][...] = vn
    vn_b = vn.astype(BF16)
    if cs > ct:
        vn_b = jnp.concatenate([vn_b.astype(F32), jnp.zeros((cs - ct, vn.shape[1]), F32)], axis=0).astype(BF16)
    tril = (lax.broadcasted_iota(jnp.int32, (ct, cs), 1) <= lax.broadcasted_iota(jnp.int32, (ct, cs), 0))
    for g in range(groups):
        sl = slice(g * gd, (g + 1) * gd)
        w = jnp.where(tril, ws_ref[g, 0:ct, :], 0.0).astype(BF16)
        f = _dot(w, vn_b[:, sl]) + bst_ref[0:ct, g:g + 1]
        sg = _gelu(zu_ref[:, sl]) * f
        merged = _sigmoid(ga_ref[:, sl]) * hm_ref[:, sl] + _sigmoid(gb_ref[:, sl]) * sg
        mg_ref[:, sl] = merged.astype(mg_ref.dtype)


def _gmlp(z3, cols, hm, lng3, lnb3, w_s, b_st, layer, ct, cs, emit_vn, merged_dtype):
    b, l, _ = z3.shape
    gw = hm.shape[-1]
    groups = w_s.shape[1]
    gd = gw // groups
    nc = l // ct

    def zspec(name):
        assert cols[name] % gw == 0
        return pl.BlockSpec((None, ct, gw), lambda i, c: (i, c, cols[name] // gw))

    in_specs = [zspec("zu"), zspec("zv"), zspec("ga"), zspec("gb"),
                pl.BlockSpec((None, ct, gw), lambda i, c: (i, c, 0)),
                pl.BlockSpec((None, 1, gw), lambda i, c: (layer, 0, 0)),
                pl.BlockSpec((None, 1, gw), lambda i, c: (layer, 0, 0)),
                pl.BlockSpec((None, groups, cs, cs), lambda i, c: (layer, 0, 0, 0)),
                pl.BlockSpec((None, cs, groups), lambda i, c: (layer, 0, 0))]
    out_specs = [pl.BlockSpec((None, ct, gw), lambda i, c: (i, c, 0))]
    out_shape = [jax.ShapeDtypeStruct((b, l, gw), merged_dtype)]
    if emit_vn:
        out_specs.append(pl.BlockSpec((None, ct, gw), lambda i, c: (i, c, 0)))
        out_shape.append(jax.ShapeDtypeStruct((b, l, gw), F32))
    return pl.pallas_call(
        functools.partial(_gmlp_body, ct, cs, groups, gd, emit_vn),
        grid=(b, nc),
        in_specs=in_specs,
        out_specs=out_specs,
        out_shape=out_shape,
        compiler_params=_cparams(2),
        name="gmlp_merge",
    )(z3, z3, z3, z3, hm, lng3, lnb3, w_s, b_st)


def _outproj_body(a_ref, w_ref, x_ref, gate_ref, o_ref):
    acc = _dot(a_ref[...].astype(BF16), w_ref[...])
    o_ref[...] = x_ref[...] + gate_ref[...] * acc.reshape(x_ref.shape)


def _outproj(merged2, w, layer, x3, mod4, k_gate, tm, tn):
    b, l, d = x3.shape
    bb, r, nt, ridx = _row_tiling(b, l, tm)
    kdim = merged2.shape[1]
    return pl.pallas_call(
        _outproj_body,
        grid=(nt, d // tn),
        in_specs=[pl.BlockSpec((tm, kdim), lambda i, j: (i, 0)),
                  pl.BlockSpec((None, kdim, tn), lambda i, j: (layer, 0, j)),
                  pl.BlockSpec((bb, r, tn), lambda i, j: ridx(i) + (j,)),
                  pl.BlockSpec((bb, None, 1, tn), lambda i, j: (ridx(i)[0], k_gate, 0, j))],
        out_specs=pl.BlockSpec((bb, r, tn), lambda i, j: ridx(i) + (j,)),
        out_shape=jax.ShapeDtypeStruct((b, l, d), F32),
        compiler_params=_cparams(2),
        name="outproj_residual",
    )(merged2, w, x3, mod4)


def _peer_pairs():
    k = P_TOPK + 1
    return [(a, b) for a in range(k) for b in range(k) if (a + 1) * (b + 1) <= k]


def _topk_body(nk, tt, pq_ref, keys_ref, st_ref, aux_ref, topv, topi, cand, cid):
    p = pl.program_id(1) % 2
    riota = lax.broadcasted_iota(jnp.int32, (nk, tt), 0).astype(F32)
    s = _dot_nt(keys_ref[...].astype(BF16), pq_ref[...].astype(BF16))
    st_ref[...] = s
    x = s
    for a in range(P_TOPK + 1):
        m = jnp.max(x, axis=0, keepdims=True)
        idx = jnp.min(jnp.where(x == m, riota, float(nk)), axis=0, keepdims=True)
        topv[p, a:a + 1, :] = m
        topi[p, a:a + 1, :] = idx
        x = jnp.where(riota == idx, -jnp.inf, x)

    @pl.when(p == 1)
    def _():
        pairs = _peer_pairs()
        npad = cand.shape[0]
        cand[len(pairs):npad, :] = jnp.full((npad - len(pairs), tt), -jnp.inf, F32)
        cid[len(pairs):npad, :] = jnp.full((npad - len(pairs), tt), 0.0, F32)
        for r, (a, b) in enumerate(pairs):
            cand[r:r + 1, :] = topv[0, a:a + 1, :] + topv[1, b:b + 1, :]
            cid[r:r + 1, :] = topi[0, a:a + 1, :] * float(nk) + topi[1, b:b + 1, :]
        x = cand[...]
        ids = cid[...]
        big = float(nk * nk)
        mx = None
        z = None
        for a in range(P_TOPK):
            m = jnp.max(x, axis=0, keepdims=True)
            e = jnp.min(jnp.where(x == m, ids, big), axis=0, keepdims=True)
            if a == 0:
                mx = m
                z = jnp.ones_like(m)
            else:
                z = z + jnp.exp(m - mx)
            x = jnp.where(ids == e, -jnp.inf, x)
        runner_up = jnp.max(x, axis=0, keepdims=True)
        aux_ref[0:1, :] = m
        aux_ref[1:2, :] = e
        aux_ref[2:3, :] = mx + jnp.log(z)
        aux_ref[3:4, :] = jnp.where(runner_up == m, 1.0, 0.0)
        aux_ref[4:8, :] = jnp.zeros((4, tt), F32)


def _peer_topk(pq, keys4, layer, tt):
    n, qw = pq.shape
    hp, nk, half = keys4.shape[1:]
    heads = hp // 2
    npairs = -(-len(_peer_pairs()) // SUBLANES) * SUBLANES
    ntop = -(-(P_TOPK + 1) // SUBLANES) * SUBLANES
    return tuple(pl.pallas_call(
        functools.partial(_topk_body, nk, tt),
        grid=(n // tt, hp),
        in_specs=[pl.BlockSpec((tt, half), lambda i, k: (i, k)),
                  pl.BlockSpec((None, None, nk, half), lambda i, k: (layer, k, 0, 0))],
        out_specs=[pl.BlockSpec((None, nk, tt), lambda i, k: (k, 0, i)),
                   pl.BlockSpec((None, SUBLANES, tt), lambda i, k: (k // 2, 0, i))],
        out_shape=[jax.ShapeDtypeStruct((hp, nk, n), F32),
                   jax.ShapeDtypeStruct((heads, SUBLANES, n), F32)],
        scratch_shapes=[pltpu.VMEM((2, ntop, tt), F32), pltpu.VMEM((2, ntop, tt), F32),
                        pltpu.VMEM((npairs, tt), F32), pltpu.VMEM((npairs, tt), F32)],
        compiler_params=_cparams(2),
        name="peer_topk",
    )(pq, keys4))


def _peer_body(heads, nk, tt, ec, tie_ref, h2_ref, st_ref, aux_ref, u_ref, v_ref, o_ref, at_scr, w_scr, p_scr):
    j = pl.program_id(1)
    rt = 4 * SUBLANES
    rows = lax.broadcasted_iota(jnp.int32, (rt, LANES), 0)
    per = ec // nk
    grp = SUBLANES // per
    base = pl.multiple_of((j // grp) * SUBLANES, SUBLANES)
    sub = j % grp

    def weights(exact_ties):
        for ii in range(per):
            i_glob = j * per + ii
            for lb in range(tt // LANES):
                ln = slice(lb * LANES, (lb + 1) * LANES)
                for r in range(nk // rt):
                    wacc = jnp.zeros((rt, LANES), F32)
                    for h in range(heads):
                        tile = st_ref[2 * h, pl.ds(base, SUBLANES), ln]
                        s0b = tile[ii:ii + 1]
                        for k in range(1, grp):
                            s0b = jnp.where(sub == k, tile[k * per + ii:k * per + ii + 1], s0b)
                        cnd = st_ref[2 * h + 1, r * rt:(r + 1) * rt, ln] + s0b
                        thr = aux_ref[h, 0:1, ln]
                        c0 = aux_ref[h, 2:3, ln]
                        if exact_ties:
                            eio = (i_glob * nk + r * rt + rows).astype(F32)
                            sel = (cnd > thr) | ((cnd == thr) & (eio <= aux_ref[h, 1:2, ln]))
                        else:
                            sel = cnd >= thr
                        wacc = wacc + jnp.where(sel, jnp.exp(cnd - c0), 0.0)
                    w_scr[ii * nk + r * rt:ii * nk + (r + 1) * rt, ln] = wacc

    tie = tie_ref[pl.program_id(0)]

    @pl.when(tie == 0)
    def _():
        weights(False)

    @pl.when(tie != 0)
    def _():
        weights(True)

    at_scr[...] = _dot_nt(u_ref[...], h2_ref[...])
    for ii in range(per):
        for lb in range(tt // LANES):
            ln = slice(lb * LANES, (lb + 1) * LANES)
            pt = w_scr[ii * nk:(ii + 1) * nk, ln] * _gelu(at_scr[ii * nk:(ii + 1) * nk, ln])
            p_scr[ln, ii * nk:(ii + 1) * nk] = pt.T.astype(BF16)

    @pl.when(j == 0)
    def _():
        o_ref[...] = jnp.zeros(o_ref.shape, F32)

    d = o_ref.shape[1]
    cw = min(d, 1024)
    for cb in range(d // cw):
        o_ref[:, cb * cw:(cb + 1) * cw] += _dot(p_scr[...], v_ref[:, cb * cw:(cb + 1) * cw])


def _peer_dense(h2, st, aux, tie, u_tab, v_tab, layer, tt, ec):
    n, d = h2.shape
    e = u_tab.shape[1]
    hp, nk, _ = st.shape
    heads = hp // 2
    assert nk == LANES and e == nk * nk and ec % nk == 0 and tt % LANES == 0
    assert SUBLANES % (ec // nk) == 0
    return pl.pallas_call(
        functools.partial(_peer_body, heads, nk, tt, ec),
        grid_spec=pltpu.PrefetchScalarGridSpec(
            num_scalar_prefetch=1,
            grid=(n // tt, e // ec),
            in_specs=[pl.BlockSpec((tt, d), lambda i, j, f: (i, 0), pipeline_mode=pl.Buffered(1)),
                      pl.BlockSpec((hp, nk, tt), lambda i, j, f: (0, 0, i), pipeline_mode=pl.Buffered(1)),
                      pl.BlockSpec((heads, SUBLANES, tt), lambda i, j, f: (0, 0, i)),
                      pl.BlockSpec((None, ec, d), lambda i, j, f: (layer, j, 0)),
                      pl.BlockSpec((None, ec, d), lambda i, j, f: (layer, j, 0))],
            out_specs=pl.BlockSpec((tt, d), lambda i, j, f: (i, 0), pipeline_mode=pl.Buffered(1)),
            scratch_shapes=[pltpu.VMEM((ec, tt), F32), pltpu.VMEM((ec, tt), F32), pltpu.VMEM((tt, ec), BF16)]),
        out_shape=jax.ShapeDtypeStruct((n, d), F32),
        compiler_params=_cparams(2),
        name="peer_dense",
    )(tie, h2, st, aux, u_tab, v_tab)


def _resid_body(final, x_ref, p_ref, gate_ref, *rest):
    x = x_ref[...] + gate_ref[...] * p_ref[...]
    if final:
        gf_ref, o_ref = rest
        o_ref[...] = x * lax.rsqrt(jnp.mean(x * x, axis=-1, keepdims=True) + EPS) * gf_ref[...]
    else:
        rest[0][...] = x


def _residual(x3, p3, mod4, k_gate, tm, g_final2=None):
    b, l, d = x3.shape
    bb, r, nt, ridx = _row_tiling(b, l, tm)
    final = g_final2 is not None
    in_specs = [pl.BlockSpec((bb, r, d), lambda i: ridx(i) + (0,)),
                pl.BlockSpec((bb, r, d), lambda i: ridx(i) + (0,)),
                pl.BlockSpec((bb, None, 1, d), lambda i: (ridx(i)[0], k_gate, 0, 0))]
    args = [x3, p3, mod4]
    if final:
        in_specs.append(pl.BlockSpec((1, d), lambda i: (0, 0)))
        args.append(g_final2)
    return pl.pallas_call(
        functools.partial(_resid_body, final),
        grid=(nt,),
        in_specs=in_specs,
        out_specs=pl.BlockSpec((bb, r, d), lambda i: ridx(i) + (0,)),
        out_shape=jax.ShapeDtypeStruct((b, l, d), F32),
        compiler_params=_cparams(1),
        name="residual",
    )(*args)


def _pick(n, pref):
    t = min(n, pref)
    while n % t:
        t //= 2
    return t


def _trunk_layer(x3, mod4, l_real, conv_hist, c0, n0, m0, state_layer, lw, layer, g_final2, want_vn,
                 prev_c=None):
    b, l, d = x3.shape
    n = b * l
    heads, dk = lw["wq"].shape[1], lw["wq"].shape[2]
    qkw = heads * dk
    cols = lw["cols"]
    chunk = LANES
    ct = min(chunk, l)
    tm = _pick(n, 512)

    z, gates = _modnorm_matmul(x3, lw["g_mix"], mod4, 1, 0, lw["w_in"], layer, tm,
                               _pick(lw["w_in"].shape[2], 1024), w_small=lw["w_gate"])
    z3 = z.reshape(b, l, -1)
    gt = jnp.swapaxes(gates[:, :2 * heads].reshape(b, l, 2 * heads), 1, 2)
    if l % chunk:
        gt = jnp.pad(gt, ((0, 0), (0, 0), (0, chunk - l % chunk)))
    conv_new = z3[:, l_real - (CONV_W - 1):l_real, cols["qk"]:cols["qk"] + qkw]
    hm, c1, n1, m1 = _mlstm(z3, cols, gt, conv_hist, lw["conv_w"], lw["conv_b"], lw["wq"], lw["wk"],
                            lw["b_i"], lw["b_f"], lw["mnorm_g"], c0, n0, m0,
                            layer, state_layer, ct, chunk, l_real, prev_c)
    outs = _gmlp(z3, cols, hm, lw["ln_g"], lw["ln_b"], lw["w_s"], lw["b_st"], layer, ct, chunk,
                 want_vn, BF16 if ct % 16 == 0 else F32)
    merged = outs[0]
    vn = outs[1] if want_vn else None
    x3 = _outproj(merged.reshape(n, -1), lw["w_out"], layer, x3, mod4, 2, tm, _pick(d, 1024))

    pq, h2 = _modnorm_matmul(x3, lw["g_ffn"], mod4, 4, 3, lw["w_pq"], layer, tm,
                             _pick(lw["w_pq"].shape[2], 512), emit_h=True)
    if l_real < l:
        pq = pq.reshape(b, l, -1)[:, :l_real].reshape(b * l_real, -1)
        h2 = h2.reshape(b, l, -1)[:, :l_real].reshape(b * l_real, -1)
    nr = b * l_real
    tt = _pick(nr, 512)
    tk = _pick(nr, 1024)
    st, aux = _peer_topk(pq, lw["keys"], layer, tk)
    tie = (jnp.max(aux[:, 3, :].reshape(-1, nr // tt, tt), axis=(0, 2)) > 0).astype(jnp.int32)
    peer = _peer_dense(h2, st, aux, tie, lw["u_tab"], lw["v_tab"], layer, tt,
                       _pick(lw["u_tab"].shape[1], 1024))
    peer = peer.reshape(b, l_real, d)
    if l_real < l:
        peer = jnp.pad(peer, ((0, 0), (0, l - l_real), (0, 0)))
    x3 = _residual(x3, peer, mod4, 5, _pick(n, 256), g_final2)
    return x3, conv_new, c1, n1, m1[:, :, 0], vn


def kernel(x_prompt, x_sample, state_C, state_n, state_m, state_conv, c_prompt, c_sample, w_ada, b_ada, g_mix, w_in, conv_w, conv_b, wq_m, wk_m, b_i, b_f, mnorm_g, ln_g, ln_b, w_s, b_s, w_out, g_ffn, w_pq, sub_keys, u_tab, v_tab, g_final):
    depth = w_ada.shape[0]
    bp, lp, d = x_prompt.shape
    bs, ls, _ = x_sample.shape
    heads, dk = wq_m.shape[1], wq_m.shape[2]
    dv = state_C.shape[-1]
    qkw, vw = heads * dk, heads * dv
    gw = ln_g.shape[1]

    ls_pad = -(-ls // SUBLANES) * SUBLANES
    xs = jnp.pad(x_sample, ((0, 0), (0, ls_pad - ls), (0, 0)))
    xp = x_prompt

    n_c = bp + bs
    n_c_pad = -(-n_c // SUBLANES) * SUBLANES
    c_all = jnp.pad(jnp.concatenate([c_prompt, c_sample], axis=0), ((0, n_c_pad - n_c), (0, 0)))

    o_v, o_o, o_g = qkw, qkw + vw, qkw + 2 * vw
    o_zu = o_g + 2 * heads
    cols = {"v": 0, "o": vw, "zu": 2 * vw, "zv": 2 * vw + gw, "ga": 2 * vw + 2 * gw,
            "gb": 2 * vw + 2 * gw + d, "qk": 2 * vw + 2 * gw + 2 * d}

    hist_pad = SUBLANES - (CONV_W - 1)
    zero_hist = jnp.zeros((bp, SUBLANES, qkw), F32)
    zero_c = jnp.zeros((1, bp, heads, dk, dv), F32)
    zero_n = jnp.zeros((1, bp, heads, dk), F32)
    zero_m = jnp.zeros((1, bp, heads, 1), F32)
    state_m4 = state_m[..., None]
    g_final2 = g_final.reshape(1, d)
    b_ada3 = b_ada[:, None, :]

    p_out = [[], [], [], []]
    s_out = [[], [], [], [], []]
    lw = {
        "cols": cols,
        "g_mix": g_mix[:, None, :], "g_ffn": g_ffn[:, None, :],
        "w_in": _regroup_w_in(jnp.swapaxes(w_in, 1, 2), qkw, vw, 2 * heads, _pick(qkw, 1024), _pick(d, 512)),
        "w_gate": jnp.pad(w_in[:, :, o_g:o_zu], ((0, 0), (0, 0), (0, LANES - 2 * heads))).astype(BF16),
        "conv_w": conv_w, "conv_b": conv_b[:, None, :],
        "wq": wq_m.astype(BF16), "wk": wk_m.astype(BF16),
        "b_i": b_i[:, :, None], "b_f": b_f[:, :, None],
        "mnorm_g": mnorm_g[:, None, :], "ln_g": ln_g[:, None, :], "ln_b": ln_b[:, None, :],
        "w_s": w_s, "b_st": jnp.swapaxes(b_s, 1, 2),
        "w_out": w_out.astype(BF16), "w_pq": w_pq.astype(BF16),
        "keys": sub_keys.reshape(depth, -1, sub_keys.shape[3], sub_keys.shape[4]),
        "u_tab": u_tab.astype(BF16), "v_tab": v_tab.astype(BF16),
    }
    for l in range(depth):
        mod = _adaln(c_all, w_ada, b_ada3, l, _pick(w_ada.shape[2], 512))
        mod_p = mod[:bp].reshape(bp, N_MOD, 1, d)
        mod_s = mod[bp:n_c].reshape(bs, N_MOD, 1, d)
        last = l == depth - 1

        xp, cv, c1, n1, m1, _ = _trunk_layer(xp, mod_p, lp, zero_hist, zero_c, zero_n, zero_m, 0,
                                             lw, l, g_final2 if last else None, False)
        for acc, val in zip(p_out, (c1, n1, m1, cv)):
            acc.append(val)

        hist_s = jnp.pad(state_conv[l], ((0, 0), (hist_pad, 0), (0, 0)))
        stack_c = last and depth == 2
        xs, cv, c1, n1, m1, vn = _trunk_layer(xs, mod_s, ls, hist_s, state_C, state_n, state_m4, l,
                                              lw, l, g_final2 if last else None, True,
                                              s_out[0][0] if stack_c else None)
        if stack_c:
            sample_c = c1
        for acc, val in zip(s_out, (c1, n1, m1, cv, vn[:, :ls])):
            acc.append(val)

    s_stacked = [jnp.stack(a) for a in s_out[1:]]
    s_stacked.insert(0, sample_c if depth == 2 else jnp.stack(s_out[0]))
    return (xp, xs[:, :ls]) + tuple(jnp.stack(a) for a in p_out) + tuple(s_stacked)
```

```python
import functools

import jax
import jax.numpy as jnp
from jax import lax
from jax.experimental import pallas as pl
from jax.experimental.pallas import tpu as pltpu

F32 = jnp.float32
BF16 = jnp.bfloat16
EPS = 1e-6
NEG_BIG = -1e30
LANES = 128
SUBLANES = 8
VMEM_LIMIT = 56 * 1024 * 1024
CONV_W = 4
P_TOPK = 16
N_MOD = 6
SQRT_HALF = 0.7071067811865476


def _cparams(n_axes):
    return pltpu.CompilerParams(dimension_semantics=("arbitrary",) * n_axes,
                                vmem_limit_bytes=VMEM_LIMIT)


def _gelu(x):
    return 0.5 * x * (1.0 + lax.erf(x * SQRT_HALF))


def _sigmoid(x):
    return 1.0 / (1.0 + jnp.exp(-x))


def _log_sigmoid(x):
    return jnp.minimum(x, 0.0) - jnp.log(1.0 + jnp.exp(-jnp.abs(x)))


def _dot(a, b):
    return jnp.dot(a, b, preferred_element_type=F32)


def _dot_nt(a, b):
    return lax.dot_general(a, b, (((1,), (1,)), ((), ())), preferred_element_type=F32)


def _dot_tn(a, b):
    return lax.dot_general(a, b, (((0,), (0,)), ((), ())), preferred_element_type=F32)


def _split3(a):
    a1 = a.astype(BF16)
    r1 = a - a1.astype(F32)
    a2 = r1.astype(BF16)
    a3 = (r1 - a2.astype(F32)).astype(BF16)
    return a1, a2, a3


def _adaln_body(c_ref, w_ref, b_ref, o_ref):
    c = c_ref[...]
    a = c * _sigmoid(c)
    a1 = a.astype(BF16)
    a2 = (a - a1.astype(F32)).astype(BF16)
    w1 = w_ref[...].astype(BF16)
    o_ref[...] = _dot(a1, w1) + _dot(a2, w1) + b_ref[...]


def _adaln(c_all, w_ada, b_ada3, layer, tn):
    m, d = c_all.shape
    n = w_ada.shape[2]
    return pl.pallas_call(
        _adaln_body,
        grid=(n // tn,),
        in_specs=[pl.BlockSpec((m, d), lambda j: (0, 0)),
                  pl.BlockSpec((None, d, tn), lambda j: (layer, 0, j)),
                  pl.BlockSpec((None, 1, tn), lambda j: (layer, 0, j))],
        out_specs=pl.BlockSpec((m, tn), lambda j: (0, j)),
        out_shape=jax.ShapeDtypeStruct((m, n), F32),
        compiler_params=_cparams(1),
        name="adaln",
    )(c_all, w_ada, b_ada3)


def _regroup_body(a_ref, o_ref):
    o_ref[...] = a_ref[...].T.astype(BF16)


def _regroup_w_in(w_in_t, qkw, vw, n_gate, tn, tr):
    depth, in_w, d = w_in_t.shape
    o_v, o_g = qkw, qkw + 2 * vw
    o_zu = o_g + n_gate
    assert (o_g - o_v) % tn == 0 and (in_w - o_zu) % tn == 0 and qkw % tn == 0 and o_zu % SUBLANES == 0
    n_a, n_b, n_c = (o_g - o_v) // tn, (in_w - o_zu) // tn, qkw // tn

    def src_map(l, r, j):
        row = jnp.where(j < n_a, o_v + j * tn,
                        jnp.where(j < n_a + n_b, o_zu + (j - n_a) * tn, (j - n_a - n_b) * tn))
        return (l, pl.multiple_of(row, SUBLANES), r * tr)

    return pl.pallas_call(
        _regroup_body,
        grid=(depth, d // tr, n_a + n_b + n_c),
        in_specs=[pl.BlockSpec((None, pl.Element(tn), pl.Element(tr)), src_map)],
        out_specs=pl.BlockSpec((None, tr, tn), lambda l, r, j: (l, r, j)),
        out_shape=jax.ShapeDtypeStruct((depth, d, (n_a + n_b + n_c) * tn), BF16),
        compiler_params=_cparams(3),
        name="regroup_w_in",
    )(w_in_t)


def _row_tiling(b, l, tm):
    if l >= tm:
        assert l % tm == 0
        per = l // tm
        return 1, tm, b * per, (lambda i: (i // per, i % per))
    assert tm % l == 0 and b % (tm // l) == 0
    bb = tm // l
    return bb, l, b // bb, (lambda i: (i, 0))


def _modnorm_body(has_small, emit_h, tm, x_ref, g_ref, sc_ref, sh_ref, w_ref, *rest):
    rest = list(rest)
    ws_ref = rest.pop(0) if has_small else None
    o_ref = rest.pop(0)
    os_ref = rest.pop(0) if has_small else None
    ho_ref = rest.pop(0) if emit_h else None
    h_scr = rest.pop(0)

    @pl.when(pl.program_id(1) == 0)
    def _():
        bb, r, d = x_ref.shape
        nsub = 4
        while bb > 1 and bb % nsub:
            nsub //= 2
        rows = tm // nsub

        def sub(k, carry):
            if bb == 1:
                sl = (slice(None), pl.ds(pl.multiple_of(k * rows, rows), rows), slice(None))
                sc, sh = sc_ref[...], sh_ref[...]
            else:
                bsub = pl.ds(pl.multiple_of(k * (bb // nsub), bb // nsub), bb // nsub)
                sl = (bsub, slice(None), slice(None))
                sc, sh = sc_ref[bsub], sh_ref[bsub]
            x = x_ref[sl]
            var = jnp.mean(x * x, axis=-1, keepdims=True)
            y = x * lax.rsqrt(var + EPS) * g_ref[...]
            h = y * (1.0 + sc) + sh
            h_scr[pl.ds(pl.multiple_of(k * rows, rows), rows), :] = h.reshape(rows, d).astype(BF16)
            return carry

        lax.fori_loop(0, nsub, sub, 0)
        if has_small:
            os_ref[...] = _dot(h_scr[...], ws_ref[...])
        if emit_h:
            ho_ref[...] = h_scr[...]

    o_ref[...] = _dot(h_scr[...], w_ref[...])


def _modnorm_matmul(x3, g3, mod4, k_scale, k_shift, w, layer, tm, tn, w_small=None, emit_h=False):
    b, l, d = x3.shape
    nout = w.shape[2]
    bb, r, nt, ridx = _row_tiling(b, l, tm)
    has_small = w_small is not None

    def xmap(i, j):
        bi, ri = ridx(i)
        return (bi, ri, 0)

    def modmap(k):
        def f(i, j):
            return (ridx(i)[0], k, 0, 0)
        return f

    in_specs = [pl.BlockSpec((bb, r, d), xmap),
                pl.BlockSpec((None, 1, d), lambda i, j: (layer, 0, 0)),
                pl.BlockSpec((bb, None, 1, d), modmap(k_scale)),
                pl.BlockSpec((bb, None, 1, d), modmap(k_shift)),
                pl.BlockSpec((None, d, tn), lambda i, j: (layer, 0, j))]
    args = [x3, g3, mod4, mod4, w]
    out_specs = [pl.BlockSpec((tm, tn), lambda i, j: (i, j))]
    out_shape = [jax.ShapeDtypeStruct((b * l, nout), F32)]
    if has_small:
        ns = w_small.shape[2]
        in_specs.append(pl.BlockSpec((None, d, ns), lambda i, j: (layer, 0, 0)))
        args.append(w_small)
        out_specs.append(pl.BlockSpec((tm, ns), lambda i, j: (i, 0)))
        out_shape.append(jax.ShapeDtypeStruct((b * l, ns), F32))
    if emit_h:
        out_specs.append(pl.BlockSpec((tm, d), lambda i, j: (i, 0)))
        out_shape.append(jax.ShapeDtypeStruct((b * l, d), BF16))
    return pl.pallas_call(
        functools.partial(_modnorm_body, has_small, emit_h, tm),
        grid=(nt, nout // tn),
        in_specs=in_specs,
        out_specs=out_specs,
        out_shape=out_shape,
        scratch_shapes=[pltpu.VMEM((tm, d), BF16)],
        compiler_params=_cparams(2),
        name="modnorm_matmul",
    )(*args)


def _mlstm_body(ct, cs, l_real, heads, dk, dv, stacked,
                qk_ref, v_ref, o_ref, gt_ref, cb_ref, cw_ref, cbias_ref, wq_ref, wk_ref,
                bi_ref, bf_ref, mg_ref, c0_ref, n0_ref, m0_ref, *rest):
    if stacked:
        prev_ref, hm_ref, call_ref, n_ref, m_ref, xbuf = rest
        c_ref = call_ref.at[1]
    else:
        hm_ref, c_ref, n_ref, m_ref, xbuf = rest
    hist = SUBLANES

    @pl.when(pl.program_id(1) == 0)
    def _():
        if stacked:
            call_ref[0] = prev_ref[...]
        c_ref[...] = c0_ref[...]
        n_ref[...] = n0_ref[...]
        m_ref[...] = jnp.broadcast_to(m0_ref[...], m_ref.shape)
        xbuf[0:hist, :] = cb_ref[...]

    xbuf[hist:hist + ct, :] = qk_ref[...]
    acc = jnp.broadcast_to(cbias_ref[...], (ct, heads * dk))
    for j in range(CONV_W):
        acc = acc + xbuf[pl.ds(hist - (CONV_W - 1) + j, ct), :] * cw_ref[j:j + 1, :]
    xc = (acc * _sigmoid(acc)).astype(BF16)
    xbuf[0:hist, :] = xbuf[ct:ct + hist, :]

    g = gt_ref[...]
    ig_all = g[0:heads] + bi_ref[...]
    lf_all = _log_sigmoid(g[heads:2 * heads] + bf_ref[...])
    if l_real < cs:
        valid = lax.broadcasted_iota(jnp.int32, (heads, cs), 1) < l_real
        ig_all = jnp.where(valid, ig_all, NEG_BIG)
        lf_all = jnp.where(valid, lf_all, 0.0)
    upper = (lax.broadcasted_iota(jnp.int32, (cs, cs), 0)
             <= lax.broadcasted_iota(jnp.int32, (cs, cs), 1)).astype(BF16)
    l1, l2, l3 = _split3(lf_all)
    f_all = _dot(l1, upper) + _dot(l2, upper) + _dot(l3, upper)

    tt = lax.broadcasted_iota(jnp.int32, (ct, cs), 0)
    ss = lax.broadcasted_iota(jnp.int32, (ct, cs), 1)
    diag = tt == ss
    tril = ss <= tt
    pad_rows = cs - ct

    m_all = m_ref[...]
    n_all = n_ref[...]

    def gates_and_projections(h):
        ig_r = ig_all[h:h + 1, :]
        f_r = f_all[h:h + 1, :]
        m0 = m_all[h:h + 1, 0:1]
        f_c = jnp.sum(jnp.where(diag, f_r, 0.0), axis=1, keepdims=True)
        ig_c = jnp.sum(jnp.where(diag, ig_r, 0.0), axis=1, keepdims=True)
        cm_c = jnp.max(jnp.where(tril, ig_r - f_r, -jnp.inf), axis=1, keepdims=True)
        m_c = f_c + jnp.maximum(m0, cm_c)
        dmat = jnp.exp(jnp.where(tril, f_c - f_r + ig_r - m_c, -jnp.inf))
        a_c = jnp.exp(f_c + m0 - m_c)
        xh = xc[:, h * dk:(h + 1) * dk]
        q = _dot(xh, wq_ref[h])
        k = _dot(xh, wk_ref[h]) * (dk ** -0.5)
        v = v_ref[:, h * dv:(h + 1) * dv]
        m_l = m_c[ct - 1:ct, :]
        f_l = f_c[ct - 1:ct, :]
        ws_c = jnp.exp(f_l - f_c + ig_c - m_l)
        a_l = jnp.exp(f_l + m0 - m_l)
        kw = k * ws_c
        if pad_rows:
            k_p = jnp.concatenate([k, jnp.zeros((pad_rows, dk), F32)], axis=0)
            kw_p = jnp.concatenate([kw, jnp.zeros((pad_rows, dk), F32)], axis=0)
            v_p = jnp.concatenate([v, jnp.zeros((pad_rows, dv), F32)], axis=0)
        else:
            k_p, kw_p, v_p = k, kw, v
        return dict(q=q, q_b=q.astype(BF16), k_b=k_p.astype(BF16), kw=kw, kw_b=kw_p.astype(BF16),
                    v_b=v_p.astype(BF16), dmat=dmat, a_c=a_c, a_l=a_l, m_c=m_c, m_l=m_l)

    def first_matmuls(h, t):
        c_old = c_ref[h]
        t["s"] = _dot_nt(t["q_b"], t["k_b"]) * t["dmat"]
        t["qc"] = _dot(t["q_b"], c_old.astype(BF16))
        return t

    def update_memory(h, t):
        c_ref[h] = t["a_l"] * c_ref[h] + _dot_tn(t["kw_b"], t["v_b"])

    def outputs(h, t):
        n_old = n_all[h:h + 1, :]
        s, a_c = t["s"], t["a_c"]
        num = a_c * t["qc"] + _dot(s.astype(BF16), t["v_b"])
        dotn = (a_c * jnp.sum(t["q"] * n_old, axis=1, keepdims=True)
                + jnp.sum(s, axis=1, keepdims=True))
        den = jnp.maximum(jnp.abs(dotn), jnp.exp(-t["m_c"])) + EPS
        hh = num / den
        hn = hh * lax.rsqrt(jnp.mean(hh * hh, axis=1, keepdims=True) + EPS)
        hn = hn * mg_ref[:, h * dv:(h + 1) * dv]
        hm_ref[:, h * dv:(h + 1) * dv] = hn * _sigmoid(o_ref[:, h * dv:(h + 1) * dv])
        n_new = t["a_l"] * n_old + jnp.sum(t["kw"], axis=0, keepdims=True)
        return n_new, jnp.broadcast_to(t["m_l"], (1, m_ref.shape[1]))

    ts = [gates_and_projections(h) for h in range(heads)]
    ts = [first_matmuls(h, ts[h]) for h in range(heads)]
    for h in range(heads):
        update_memory(h, ts[h])
    new = [outputs(h, ts[h]) for h in range(heads)]

    for h in range(heads):
        n_ref[h:h + 1, :] = new[h][0]
        m_ref[h:h + 1, :] = new[h][1]


def _mlstm(z3, cols, gt, conv_hist, conv_w, conv_b3, wq, wk, bi3, bf3, mg3, c0, n0, m0,
           layer, state_layer, ct, cs, l_real, prev_c=None):
    b, l, _ = z3.shape
    heads, dk = wq.shape[1], wq.shape[2]
    dv = c0.shape[-1]
    qkw, vw = heads * dk, heads * dv
    nc = l // ct
    stacked = prev_c is not None
    assert cols["qk"] % qkw == 0 and cols["v"] % vw == 0 and cols["o"] % vw == 0
    body = functools.partial(_mlstm_body, ct, cs, l_real, heads, dk, dv, stacked)
    in_specs = [
        pl.BlockSpec((None, ct, qkw), lambda i, c: (i, c, cols["qk"] // qkw)),
        pl.BlockSpec((None, ct, vw), lambda i, c: (i, c, cols["v"] // vw)),
        pl.BlockSpec((None, ct, vw), lambda i, c: (i, c, cols["o"] // vw)),
        pl.BlockSpec((None, 2 * heads, cs), lambda i, c: (i, 0, c)),
        pl.BlockSpec((None, SUBLANES, qkw), lambda i, c: (i, 0, 0)),
        pl.BlockSpec((None, CONV_W, qkw), lambda i, c: (layer, 0, 0)),
        pl.BlockSpec((None, 1, qkw), lambda i, c: (layer, 0, 0)),
        pl.BlockSpec((None, heads, dk, dk), lambda i, c: (layer, 0, 0, 0)),
        pl.BlockSpec((None, heads, dk, dk), lambda i, c: (layer, 0, 0, 0)),
        pl.BlockSpec((None, heads, 1), lambda i, c: (layer, 0, 0)),
        pl.BlockSpec((None, heads, 1), lambda i, c: (layer, 0, 0)),
        pl.BlockSpec((None, 1, vw), lambda i, c: (layer, 0, 0)),
        pl.BlockSpec((None, None, heads, dk, dv), lambda i, c: (state_layer, i, 0, 0, 0)),
        pl.BlockSpec((None, None, heads, dk), lambda i, c: (state_layer, i, 0, 0)),
        pl.BlockSpec((None, None, heads, 1), lambda i, c: (state_layer, i, 0, 0)),
    ]
    out_specs = [
        pl.BlockSpec((None, ct, vw), lambda i, c: (i, c, 0)),
        pl.BlockSpec((None, heads, dk, dv), lambda i, c: (i, 0, 0, 0)),
        pl.BlockSpec((None, heads, dk), lambda i, c: (i, 0, 0)),
        pl.BlockSpec((None, heads, LANES), lambda i, c: (i, 0, 0)),
    ]
    out_shape = [
        jax.ShapeDtypeStruct((b, l, vw), F32),
        jax.ShapeDtypeStruct((b, heads, dk, dv), F32),
        jax.ShapeDtypeStruct((b, heads, dk), F32),
        jax.ShapeDtypeStruct((b, heads, LANES), F32),
    ]
    args = [z3, z3, z3, gt, conv_hist, conv_w, conv_b3, wq, wk, bi3, bf3, mg3, c0, n0, m0]
    if stacked:
        in_specs.append(pl.BlockSpec((None, heads, dk, dv), lambda i, c: (i, 0, 0, 0)))
        args.append(prev_c)
        out_specs[1] = pl.BlockSpec((2, None, heads, dk, dv), lambda i, c: (0, i, 0, 0, 0))
        out_shape[1] = jax.ShapeDtypeStruct((2, b, heads, dk, dv), F32)
    return pl.pallas_call(
        body,
        grid=(b, nc),
        in_specs=in_specs,
        out_specs=out_specs,
        out_shape=out_shape,
        scratch_shapes=[pltpu.VMEM((ct + SUBLANES, qkw), F32)],
        compiler_params=_cparams(2),
        name="mlstm",
    )(*args)


def _gmlp_body(ct, cs, groups, gd, emit_vn, zu_ref, zv_ref, ga_ref, gb_ref, hm_ref, lng_ref, lnb_ref,
               ws_ref, bst_ref, *outs):
    mg_ref = outs[0]
    gv = _gelu(zv_ref[...])
    mu = jnp.mean(gv, axis=-1, keepdims=True)
    xc = gv - mu
    var = jnp.mean(xc * xc, axis=-1, keepdims=True)
    vn = xc * lax.rsqrt(var + EPS) * lng_ref[...] + lnb_ref[...]
    if emit_vn:
        outs[1][...] = vn
    vn_b = vn.astype(BF16)
    if cs > ct:
        vn_b = jnp.concatenate([vn_b.astype(F32), jnp.zeros((cs - ct, vn.shape[1]), F32)], axis=0).astype(BF16)
    tril = (lax.broadcasted_iota(jnp.int32, (ct, cs), 1) <= lax.broadcasted_iota(jnp.int32, (ct, cs), 0))
    for g in range(groups):
        sl = slice(g * gd, (g + 1) * gd)
        w = jnp.where(tril, ws_ref[g, 0:ct, :], 0.0).astype(BF16)
        f = _dot(w, vn_b[:, sl]) + bst_ref[0:ct, g:g + 1]
        sg = _gelu(zu_ref[:, sl]) * f
        merged = _sigmoid(ga_ref[:, sl]) * hm_ref[:, sl] + _sigmoid(gb_ref[:, sl]) * sg
        mg_ref[:, sl] = merged.astype(mg_ref.dtype)


def _gmlp(z3, cols, hm, lng3, lnb3, w_s, b_st, layer, ct, cs, emit_vn, merged_dtype):
    b, l, _ = z3.shape
    gw = hm.shape[-1]
    groups = w_s.shape[1]
    gd = gw // groups
    nc = l // ct

    def zspec(name):
        assert cols[name] % gw == 0
        return pl.BlockSpec((None, ct, gw), lambda i, c: (i, c, cols[name] // gw))

    in_specs = [zspec("zu"), zspec("zv"), zspec("ga"), zspec("gb"),
                pl.BlockSpec((None, ct, gw), lambda i, c: (i, c, 0)),
                pl.BlockSpec((None, 1, gw), lambda i, c: (layer, 0, 0)),
                pl.BlockSpec((None, 1, gw), lambda i, c: (layer, 0, 0)),
                pl.BlockSpec((None, groups, cs, cs), lambda i, c: (layer, 0, 0, 0)),
                pl.BlockSpec((None, cs, groups), lambda i, c: (layer, 0, 0))]
    out_specs = [pl.BlockSpec((None, ct, gw), lambda i, c: (i, c, 0))]
    out_shape = [jax.ShapeDtypeStruct((b, l, gw), merged_dtype)]
    if emit_vn:
        out_specs.append(pl.BlockSpec((None, ct, gw), lambda i, c: (i, c, 0)))
        out_shape.append(jax.ShapeDtypeStruct((b, l, gw), F32))
    return pl.pallas_call(
        functools.partial(_gmlp_body, ct, cs, groups, gd, emit_vn),
        grid=(b, nc),
        in_specs=in_specs,
        out_specs=out_specs,
        out_shape=out_shape,
        compiler_params=_cparams(2),
        name="gmlp_merge",
    )(z3, z3, z3, z3, hm, lng3, lnb3, w_s, b_st)


def _outproj_body(a_ref, w_ref, x_ref, gate_ref, o_ref):
    acc = _dot(a_ref[...].astype(BF16), w_ref[...])
    o_ref[...] = x_ref[...] + gate_ref[...] * acc.reshape(x_ref.shape)


def _outproj(merged2, w, layer, x3, mod4, k_gate, tm, tn):
    b, l, d = x3.shape
    bb, r, nt, ridx = _row_tiling(b, l, tm)
    kdim = merged2.shape[1]
    return pl.pallas_call(
        _outproj_body,
        grid=(nt, d // tn),
        in_specs=[pl.BlockSpec((tm, kdim), lambda i, j: (i, 0)),
                  pl.BlockSpec((None, kdim, tn), lambda i, j: (layer, 0, j)),
                  pl.BlockSpec((bb, r, tn), lambda i, j: ridx(i) + (j,)),
                  pl.BlockSpec((bb, None, 1, tn), lambda i, j: (ridx(i)[0], k_gate, 0, j))],
        out_specs=pl.BlockSpec((bb, r, tn), lambda i, j: ridx(i) + (j,)),
        out_shape=jax.ShapeDtypeStruct((b, l, d), F32),
        compiler_params=_cparams(2),
        name="outproj_residual",
    )(merged2, w, x3, mod4)


def _peer_pairs():
    k = P_TOPK + 1
    return [(a, b) for a in range(k) for b in range(k) if (a + 1) * (b + 1) <= k]


def _topk_body(nk, tt, pq_ref, keys_ref, st_ref, aux_ref, topv, topi, cand, cid):
    p = pl.program_id(1) % 2
    riota = lax.broadcasted_iota(jnp.int32, (nk, tt), 0).astype(F32)
    s = _dot_nt(keys_ref[...].astype(BF16), pq_ref[...].astype(BF16))
    st_ref[...] = s
    x = s
    for a in range(P_TOPK + 1):
        m = jnp.max(x, axis=0, keepdims=True)
        idx = jnp.min(jnp.where(x == m, riota, float(nk)), axis=0, keepdims=True)
        topv[p, a:a + 1, :] = m
        topi[p, a:a + 1, :] = idx
        x = jnp.where(riota == idx, -jnp.inf, x)

    @pl.when(p == 1)
    def _():
        pairs = _peer_pairs()
        npad = cand.shape[0]
        cand[len(pairs):npad, :] = jnp.full((npad - len(pairs), tt), -jnp.inf, F32)
        cid[len(pairs):npad, :] = jnp.full((npad - len(pairs), tt), 0.0, F32)
        for r, (a, b) in enumerate(pairs):
            cand[r:r + 1, :] = topv[0, a:a + 1, :] + topv[1, b:b + 1, :]
            cid[r:r + 1, :] = topi[0, a:a + 1, :] * float(nk) + topi[1, b:b + 1, :]
        x = cand[...]
        ids = cid[...]
        big = float(nk * nk)
        mx = None
        z = None
        for a in range(P_TOPK):
            m = jnp.max(x, axis=0, keepdims=True)
            e = jnp.min(jnp.where(x == m, ids, big), axis=0, keepdims=True)
            if a == 0:
                mx = m
                z = jnp.ones_like(m)
            else:
                z = z + jnp.exp(m - mx)
            x = jnp.where(ids == e, -jnp.inf, x)
        runner_up = jnp.max(x, axis=0, keepdims=True)
        aux_ref[0:1, :] = m
        aux_ref[1:2, :] = e
        aux_ref[2:3, :] = mx + jnp.log(z)
        aux_ref[3:4, :] = jnp.where(runner_up == m, 1.0, 0.0)
        aux_ref[4:8, :] = jnp.zeros((4, tt), F32)


def _peer_topk(pq, keys4, layer, tt):
    n, qw = pq.shape
    hp, nk, half = keys4.shape[1:]
    heads = hp // 2
    npairs = -(-len(_peer_pairs()) // SUBLANES) * SUBLANES
    ntop = -(-(P_TOPK + 1) // SUBLANES) * SUBLANES
    return tuple(pl.pallas_call(
        functools.partial(_topk_body, nk, tt),
        grid=(n // tt, hp),
        in_specs=[pl.BlockSpec((tt, half), lambda i, k: (i, k)),
                  pl.BlockSpec((None, None, nk, half), lambda i, k: (layer, k, 0, 0))],
        out_specs=[pl.BlockSpec((None, nk, tt), lambda i, k: (k, 0, i)),
                   pl.BlockSpec((None, SUBLANES, tt), lambda i, k: (k // 2, 0, i))],
        out_shape=[jax.ShapeDtypeStruct((hp, nk, n), F32),
                   jax.ShapeDtypeStruct((heads, SUBLANES, n), F32)],
        scratch_shapes=[pltpu.VMEM((2, ntop, tt), F32), pltpu.VMEM((2, ntop, tt), F32),
                        pltpu.VMEM((npairs, tt), F32), pltpu.VMEM((npairs, tt), F32)],
        compiler_params=_cparams(2),
        name="peer_topk",
    )(pq, keys4))


def _peer_body(heads, nk, tt, ec, tie_ref, h2_ref, st_ref, aux_ref, u_ref, v_ref, o_ref, at_scr, w_scr, p_scr):
    j = pl.program_id(1)
    rt = 4 * SUBLANES
    rows = lax.broadcasted_iota(jnp.int32, (rt, LANES), 0)
    per = ec // nk
    grp = SUBLANES // per
    base = pl.multiple_of((j // grp) * SUBLANES, SUBLANES)
    sub = j % grp

    def weights(exact_ties):
        for ii in range(per):
            i_glob = j * per + ii
            for lb in range(tt // LANES):
                ln = slice(lb * LANES, (lb + 1) * LANES)
                for r in range(nk // rt):
                    wacc = jnp.zeros((rt, LANES), F32)
                    for h in range(heads):
                        tile = st_ref[2 * h, pl.ds(base, SUBLANES), ln]
                        s0b = tile[ii:ii + 1]
                        for k in range(1, grp):
                            s0b = jnp.where(sub == k, tile[k * per + ii:k * per + ii + 1], s0b)
                        cnd = st_ref[2 * h + 1, r * rt:(r + 1) * rt, ln] + s0b
                        thr = aux_ref[h, 0:1, ln]
                        c0 = aux_ref[h, 2:3, ln]
                        if exact_ties:
                            eio = (i_glob * nk + r * rt + rows).astype(F32)
                            sel = (cnd > thr) | ((cnd == thr) & (eio <= aux_ref[h, 1:2, ln]))
                        else:
                            sel = cnd >= thr
                        wacc = wacc + jnp.where(sel, jnp.exp(cnd - c0), 0.0)
                    w_scr[ii * nk + r * rt:ii * nk + (r + 1) * rt, ln] = wacc

    tie = tie_ref[pl.program_id(0)]

    @pl.when(tie == 0)
    def _():
        weights(False)

    @pl.when(tie != 0)
    def _():
        weights(True)

    at_scr[...] = _dot_nt(u_ref[...], h2_ref[...])
    for ii in range(per):
        for lb in range(tt // LANES):
            ln = slice(lb * LANES, (lb + 1) * LANES)
            pt = w_scr[ii * nk:(ii + 1) * nk, ln] * _gelu(at_scr[ii * nk:(ii + 1) * nk, ln])
            p_scr[ln, ii * nk:(ii + 1) * nk] = pt.T.astype(BF16)

    @pl.when(j == 0)
    def _():
        o_ref[...] = jnp.zeros(o_ref.shape, F32)

    d = o_ref.shape[1]
    cw = min(d, 1024)
    for cb in range(d // cw):
        o_ref[:, cb * cw:(cb + 1) * cw] += _dot(p_scr[...], v_ref[:, cb * cw:(cb + 1) * cw])


def _peer_dense(h2, st, aux, tie, u_tab, v_tab, layer, tt, ec):
    n, d = h2.shape
    e = u_tab.shape[1]
    hp, nk, _ = st.shape
    heads = hp // 2
    assert nk == LANES and e == nk * nk and ec % nk == 0 and tt % LANES == 0
    assert SUBLANES % (ec // nk) == 0
    return pl.pallas_call(
        functools.partial(_peer_body, heads, nk, tt, ec),
        grid_spec=pltpu.PrefetchScalarGridSpec(
            num_scalar_prefetch=1,
            grid=(n // tt, e // ec),
            in_specs=[pl.BlockSpec((tt, d), lambda i, j, f: (i, 0), pipeline_mode=pl.Buffered(1)),
                      pl.BlockSpec((hp, nk, tt), lambda i, j, f: (0, 0, i), pipeline_mode=pl.Buffered(1)),
                      pl.BlockSpec((heads, SUBLANES, tt), lambda i, j, f: (0, 0, i)),
                      pl.BlockSpec((None, ec, d), lambda i, j, f: (layer, j, 0)),
                      pl.BlockSpec((None, ec, d), lambda i, j, f: (layer, j, 0))],
            out_specs=pl.BlockSpec((tt, d), lambda i, j, f: (i, 0), pipeline_mode=pl.Buffered(1)),
            scratch_shapes=[pltpu.VMEM((ec, tt), F32), pltpu.VMEM((ec, tt), F32), pltpu.VMEM((tt, ec), BF16)]),
        out_shape=jax.ShapeDtypeStruct((n, d), F32),
        compiler_params=_cparams(2),
        name="peer_dense",
    )(tie, h2, st, aux, u_tab, v_tab)


def _resid_body(final, x_ref, p_ref, gate_ref, *rest):
    x = x_ref[...] + gate_ref[...] * p_ref[...]
    if final:
        gf_ref, o_ref = rest
        o_ref[...] = x * lax.rsqrt(jnp.mean(x * x, axis=-1, keepdims=True) + EPS) * gf_ref[...]
    else:
        rest[0][...] = x


def _residual(x3, p3, mod4, k_gate, tm, g_final2=None):
    b, l, d = x3.shape
    bb, r, nt, ridx = _row_tiling(b, l, tm)
    final = g_final2 is not None
    in_specs = [pl.BlockSpec((bb, r, d), lambda i: ridx(i) + (0,)),
                pl.BlockSpec((bb, r, d), lambda i: ridx(i) + (0,)),
                pl.BlockSpec((bb, None, 1, d), lambda i: (ridx(i)[0], k_gate, 0, 0))]
    args = [x3, p3, mod4]
    if final:
        in_specs.append(pl.BlockSpec((1, d), lambda i: (0, 0)))
        args.append(g_final2)
    return pl.pallas_call(
        functools.partial(_resid_body, final),
        grid=(nt,),
        in_specs=in_specs,
        out_specs=pl.BlockSpec((bb, r, d), lambda i: ridx(i) + (0,)),
        out_shape=jax.ShapeDtypeStruct((b, l, d), F32),
        compiler_params=_cparams(1),
        name="residual",
    )(*args)


def _pick(n, pref):
    t = min(n, pref)
    while n % t:
        t //= 2
    return t


def _trunk_layer(x3, mod4, l_real, conv_hist, c0, n0, m0, state_layer, lw, layer, g_final2, want_vn,
                 prev_c=None):
    b, l, d = x3.shape
    n = b * l
    heads, dk = lw["wq"].shape[1], lw["wq"].shape[2]
    qkw = heads * dk
    cols = lw["cols"]
    chunk = LANES
    ct = min(chunk, l)
    tm = _pick(n, 512)

    z, gates = _modnorm_matmul(x3, lw["g_mix"], mod4, 1, 0, lw["w_in"], layer, tm,
                               _pick(lw["w_in"].shape[2], 1024), w_small=lw["w_gate"])
    z3 = z.reshape(b, l, -1)
    gt = jnp.swapaxes(gates[:, :2 * heads].reshape(b, l, 2 * heads), 1, 2)
    if l % chunk:
        gt = jnp.pad(gt, ((0, 0), (0, 0), (0, chunk - l % chunk)))
    conv_new = z3[:, l_real - (CONV_W - 1):l_real, cols["qk"]:cols["qk"] + qkw]
    hm, c1, n1, m1 = _mlstm(z3, cols, gt, conv_hist, lw["conv_w"], lw["conv_b"], lw["wq"], lw["wk"],
                            lw["b_i"], lw["b_f"], lw["mnorm_g"], c0, n0, m0,
                            layer, state_layer, ct, chunk, l_real, prev_c)
    outs = _gmlp(z3, cols, hm, lw["ln_g"], lw["ln_b"], lw["w_s"], lw["b_st"], layer, ct, chunk,
                 want_vn, BF16 if ct % 16 == 0 else F32)
    merged = outs[0]
    vn = outs[1] if want_vn else None
    x3 = _outproj(merged.reshape(n, -1), lw["w_out"], layer, x3, mod4, 2, tm, _pick(d, 1024))

    pq, h2 = _modnorm_matmul(x3, lw["g_ffn"], mod4, 4, 3, lw["w_pq"], layer, tm,
                             _pick(lw["w_pq"].shape[2], 512), emit_h=True)
    if l_real < l:
        pq = pq.reshape(b, l, -1)[:, :l_real].reshape(b * l_real, -1)
        h2 = h2.reshape(b, l, -1)[:, :l_real].reshape(b * l_real, -1)
    nr = b * l_real
    tt = _pick(nr, 512)
    tk = _pick(nr, 1024)
    st, aux = _peer_topk(pq, lw["keys"], layer, tk)
    tie = (jnp.max(aux[:, 3, :].reshape(-1, nr // tt, tt), axis=(0, 2)) > 0).astype(jnp.int32)
    peer = _peer_dense(h2, st, aux, tie, lw["u_tab"], lw["v_tab"], layer, tt,
                       _pick(lw["u_tab"].shape[1], 1024))
    peer = peer.reshape(b, l_real, d)
    if l_real < l:
        peer = jnp.pad(peer, ((0, 0), (0, l - l_real), (0, 0)))
    x3 = _residual(x3, peer, mod4, 5, _pick(n, 256), g_final2)
    return x3, conv_new, c1, n1, m1[:, :, 0], vn


def kernel(x_prompt, x_sample, state_C, state_n, state_m, state_conv, c_prompt, c_sample, w_ada, b_ada, g_mix, w_in, conv_w, conv_b, wq_m, wk_m, b_i, b_f, mnorm_g, ln_g, ln_b, w_s, b_s, w_out, g_ffn, w_pq, sub_keys, u_tab, v_tab, g_final):
    depth = w_ada.shape[0]
    bp, lp, d = x_prompt.shape
    bs, ls, _ = x_sample.shape
    heads, dk = wq_m.shape[1], wq_m.shape[2]
    dv = state_C.shape[-1]
    qkw, vw = heads * dk, heads * dv
    gw = ln_g.shape[1]

    ls_pad = -(-ls // SUBLANES) * SUBLANES
    xs = jnp.pad(x_sample, ((0, 0), (0, ls_pad - ls), (0, 0)))
    xp = x_prompt

    n_c = bp + bs
    n_c_pad = -(-n_c // SUBLANES) * SUBLANES
    c_all = jnp.pad(jnp.concatenate([c_prompt, c_sample], axis=0), ((0, n_c_pad - n_c), (0, 0)))

    o_v, o_o, o_g = qkw, qkw + vw, qkw + 2 * vw
    o_zu = o_g + 2 * heads
    cols = {"v": 0, "o": vw, "zu": 2 * vw, "zv": 2 * vw + gw, "ga": 2 * vw + 2 * gw,
            "gb": 2 * vw + 2 * gw + d, "qk": 2 * vw + 2 * gw + 2 * d}

    hist_pad = SUBLANES - (CONV_W - 1)
    zero_hist = jnp.zeros((bp, SUBLANES, qkw), F32)
    zero_c = jnp.zeros((1, bp, heads, dk, dv), F32)
    zero_n = jnp.zeros((1, bp, heads, dk), F32)
    zero_m = jnp.zeros((1, bp, heads, 1), F32)
    state_m4 = state_m[..., None]
    g_final2 = g_final.reshape(1, d)
    b_ada3 = b_ada[:, None, :]

    p_out = [[], [], [], []]
    s_out = [[], [], [], [], []]
    lw = {
        "cols": cols,
        "g_mix": g_mix[:, None, :], "g_ffn": g_ffn[:, None, :],
        "w_in": _regroup_w_in(jnp.swapaxes(w_in, 1, 2), qkw, vw, 2 * heads, _pick(qkw, 1024), _pick(d, 512)),
        "w_gate": jnp.pad(w_in[:, :, o_g:o_zu], ((0, 0), (0, 0), (0, LANES - 2 * heads))).astype(BF16),
        "conv_w": conv_w, "conv_b": conv_b[:, None, :],
        "wq": wq_m.astype(BF16), "wk": wk_m.astype(BF16),
        "b_i": b_i[:, :, None], "b_f": b_f[:, :, None],
        "mnorm_g": mnorm_g[:, None, :], "ln_g": ln_g[:, None, :], "ln_b": ln_b[:, None, :],
        "w_s": w_s, "b_st": jnp.swapaxes(b_s, 1, 2),
        "w_out": w_out.astype(BF16), "w_pq": w_pq.astype(BF16),
        "keys": sub_keys.reshape(depth, -1, sub_keys.shape[3], sub_keys.shape[4]),
        "u_tab": u_tab.astype(BF16), "v_tab": v_tab.astype(BF16),
    }
    for l in range(depth):
        mod = _adaln(c_all, w_ada, b_ada3, l, _pick(w_ada.shape[2], 512))
        mod_p = mod[:bp].reshape(bp, N_MOD, 1, d)
        mod_s = mod[bp:n_c].reshape(bs, N_MOD, 1, d)
        last = l == depth - 1

        xp, cv, c1, n1, m1, _ = _trunk_layer(xp, mod_p, lp, zero_hist, zero_c, zero_n, zero_m, 0,
                                             lw, l, g_final2 if last else None, False)
        for acc, val in zip(p_out, (c1, n1, m1, cv)):
            acc.append(val)

        hist_s = jnp.pad(state_conv[l], ((0, 0), (hist_pad, 0), (0, 0)))
        stack_c = last and depth == 2
        xs, cv, c1, n1, m1, vn = _trunk_layer(xs, mod_s, ls, hist_s, state_C, state_n, state_m4, l,
                                              lw, l, g_final2 if last else None, True,
                                              s_out[0][0] if stack_c else None)
        if stack_c:
            sample_c = c1
        for acc, val in zip(s_out, (c1, n1, m1, cv, vn[:, :ls])):
            acc.append(val)

    s_stacked = [jnp.stack(a) for a in s_out[1:]]
    s_stacked.insert(0, sample_c if depth == 2 else jnp.stack(s_out[0]))
    return (xp, xs[:, :ls]) + tuple(jnp.stack(a) for a in p_out) + tuple(s_stacked)
```
